```python
import jax
import jax.numpy as jnp
from jax import lax
import numpy as np

D_MODEL = 2048
BATCH = 2
SEQ = 4096
DEPTH = 1

M_HEADS = 4
A_HEADS = 8
M_V_DIM = D_MODEL // 2 // M_HEADS
M_QK_DIM = M_V_DIM // 2
A_HEAD_DIM = D_MODEL // 2 // A_HEADS
M_CHUNK = 64
CONV_WIDTH = 4
GATE_CAP = 15.0
ATTN_PATTERNS = ((128, 1), (512, 4), (2048, 16))
ROPE_THETA = 10000.0
NORM_EPS = 1e-6
D_FF = ((8 * D_MODEL + 3 * 256 - 1) // (3 * 256)) * 256
M_QK_WIDTH = M_HEADS * M_QK_DIM
M_V_WIDTH = M_HEADS * M_V_DIM
A_WIDTH = A_HEADS * A_HEAD_DIM
MIX_WIDTH = M_V_WIDTH + A_WIDTH
SPLIT_SIZES = (M_QK_WIDTH, M_QK_WIDTH, M_V_WIDTH, M_V_WIDTH, M_HEADS, M_HEADS, A_WIDTH, A_WIDTH, A_WIDTH)
IN_WIDTH = 2 * M_QK_WIDTH + 2 * M_V_WIDTH + 2 * M_HEADS + 3 * A_WIDTH

kernel_name = "hymba_mlstm_dilated_swa_swiglu"


def rms_norm(x, w):
    xf = x.astype(jnp.float32)
    y = xf * lax.rsqrt(jnp.mean(xf * xf, axis=-1, keepdims=True) + NORM_EPS)
    return y * w.astype(jnp.float32)


def to_heads(t, n_heads):
    b, s, _ = t.shape
    return t.reshape(b, s, n_heads, -1).transpose(0, 2, 1, 3)


def from_heads(t):
    b, h, s, d = t.shape
    return t.transpose(0, 2, 1, 3).reshape(b, s, h * d)


def soft_cap(z):
    return GATE_CAP * jnp.tanh(z / GATE_CAP)


def causal_short_conv(x, w, b):
    k_width = w.shape[0]
    s = x.shape[1]
    xp = jnp.pad(x, ((0, 0), (k_width - 1, 0), (0, 0)))
    y = b
    for j in range(k_width):
        y = y + w[j] * xp[:, j:j + s]
    return y


def rope_tables(s, dim):
    inv_freq = ROPE_THETA ** (-jnp.arange(0, dim, 2, dtype=jnp.float32) / dim)
    ang = jnp.arange(s, dtype=jnp.float32)[:, None] * inv_freq[None, :]
    return jnp.cos(ang), jnp.sin(ang)


def apply_rope(x, cos, sin):
    half = x.shape[-1] // 2
    x1, x2 = x[..., :half], x[..., half:]
    return jnp.concatenate([x1 * cos - x2 * sin, x2 * cos + x1 * sin], axis=-1)


def mlstm_chunkwise(q, k, v, i_pre, f_pre):
    b, h, s, dk = q.shape
    dv = v.shape[-1]
    L = M_CHUNK
    nc = s // L
    q = q.reshape(b, h, nc, L, dk) * (dk ** -0.5)
    k = k.reshape(b, h, nc, L, dk)
    v = v.reshape(b, h, nc, L, dv)
    log_i = i_pre.reshape(b, h, nc, L)
    log_f = jax.nn.log_sigmoid(f_pre).reshape(b, h, nc, L)
    cum_f = jnp.cumsum(log_f, axis=-1)
    g = cum_f[..., -1]
    a = g[..., None] - cum_f + log_i
    m_loc = jnp.max(a, axis=-1)
    w_state = jnp.exp(a - m_loc[..., None])
    kv_c = jnp.einsum('bhcl,bhcld,bhcle->bhcde', w_state, k, v)
    n_c = jnp.einsum('bhcl,bhcld->bhcd', w_state, k)

    def step(carry, inp):
        c_st, n_st, m_st = carry
        kv, nn, ml, gg = inp
        m_new = jnp.maximum(gg + m_st, ml)
        s_old = jnp.exp(gg + m_st - m_new)
        s_new = jnp.exp(ml - m_new)
        c_new = s_old[..., None, None] * c_st + s_new[..., None, None] * kv
        n_new = s_old[..., None] * n_st + s_new[..., None] * nn
        return (c_new, n_new, m_new), (c_st, n_st, m_st)

    init = (jnp.zeros((b, h, dk, dv), jnp.float32), jnp.zeros((b, h, dk), jnp.float32),
            jnp.zeros((b, h), jnp.float32))
    xs = (jnp.moveaxis(kv_c, 2, 0), jnp.moveaxis(n_c, 2, 0), jnp.moveaxis(m_loc, 2, 0), jnp.moveaxis(g, 2, 0))
    _, (c_prev, n_prev, m_prev) = lax.scan(step, init, xs)
    c_prev = jnp.moveaxis(c_prev, 0, 2)
    n_prev = jnp.moveaxis(n_prev, 0, 2)
    m_prev = jnp.moveaxis(m_prev, 0, 2)

    causal = jnp.tril(jnp.ones((L, L), dtype=bool))
    d_log = cum_f[..., :, None] - cum_f[..., None, :] + log_i[..., None, :]
    d_log = jnp.where(causal, d_log, -jnp.inf)
    inter_log = cum_f + m_prev[..., None]
    m_t = jnp.maximum(inter_log, jnp.max(d_log, axis=-1))
    p_intra = jnp.exp(d_log - m_t[..., None])
    w_inter = jnp.exp(inter_log - m_t)
    s_qk = jnp.einsum('bhctd,bhcsd->bhcts', q, k) * p_intra
    num = (w_inter[..., None] * jnp.einsum('bhctd,bhcde->bhcte', q, c_prev)
           + jnp.einsum('bhcts,bhcse->bhcte', s_qk, v))
    den = w_inter * jnp.einsum('bhctd,bhcd->bhct', q, n_prev) + jnp.sum(s_qk, axis=-1)
    out = num / jnp.maximum(jnp.abs(den), jnp.exp(-m_t))[..., None]
    return out.reshape(b, h, s, dv)


def dilated_window_attention(q, k, v, window, dilation):
    b, h, s, dh = q.shape
    w_sub = window // dilation
    L = s // dilation
    nb = -(-L // w_sub)
    lp = nb * w_sub

    def to_sub(t):
        t = t.reshape(b, h, L, dilation, dh).transpose(0, 1, 3, 2, 4)
        return jnp.pad(t, ((0, 0), (0, 0), (0, 0), (0, lp - L), (0, 0)))

    def band(t):
        t = jnp.pad(t, ((0, 0), (0, 0), (0, 0), (w_sub, 0), (0, 0))).reshape(b, h, dilation, nb + 1, w_sub, dh)
        return jnp.concatenate([t[:, :, :, :-1], t[:, :, :, 1:]], axis=4)

    qb = to_sub(q).reshape(b, h, dilation, nb, w_sub, dh)
    kb = band(to_sub(k))
    vb = band(to_sub(v))
    scores = jnp.einsum('bhrnqd,bhrnkd->bhrnqk', qb, kb) * (dh ** -0.5)
    qi = jnp.arange(w_sub)[:, None]
    kj = jnp.arange(2 * w_sub)[None, :]
    blk = jnp.arange(nb)[:, None, None]
    dist = qi + w_sub - kj
    valid = (dist >= 0) & (dist <= w_sub) & (blk * w_sub + kj - w_sub >= 0)
    scores = jnp.where(valid, scores, -jnp.inf)
    m = jnp.max(scores, axis=-1, keepdims=True)
    p = jnp.exp(scores - m)
    den = jnp.sum(p, axis=-1)
    o = jnp.einsum('bhrnqk,bhrnkd->bhrnqd', p, vb) / den[..., None]
    lse = m[..., 0] + jnp.log(den)
    o = o.reshape(b, h, dilation, lp, dh)[:, :, :, :L].transpose(0, 1, 3, 2, 4).reshape(b, h, s, dh)
    lse = lse.reshape(b, h, dilation, lp)[:, :, :, :L].transpose(0, 1, 3, 2).reshape(b, h, s)
    return o, lse


def setup_inputs(seed: int = 0) -> dict:
    key = jax.random.key(seed)
    ks = jax.random.split(key, 20)
    f32 = jnp.float32

    def nrm(k, shape, scale):
        return jax.random.normal(k, shape, f32) * scale

    return {
        "x": nrm(ks[0], (BATCH, SEQ, D_MODEL), 1.0),
        "norm1_w": 1.0 + nrm(ks[1], (DEPTH, D_MODEL), 0.02),
        "w_in": nrm(ks[2], (DEPTH, D_MODEL, IN_WIDTH), D_MODEL ** -0.5),
        "conv_w": nrm(ks[3], (DEPTH, CONV_WIDTH, 2 * M_QK_WIDTH), CONV_WIDTH ** -0.5),
        "conv_b": nrm(ks[4], (DEPTH, 2 * M_QK_WIDTH), 0.01),
        "igate_b": nrm(ks[5], (DEPTH, M_HEADS), 0.1),
        "fgate_b": jnp.linspace(3.0, 6.0, M_HEADS, dtype=f32) + nrm(ks[6], (DEPTH, M_HEADS), 0.1),
        "q_norm_w": 1.0 + nrm(ks[7], (DEPTH, A_HEAD_DIM), 0.02),
        "k_norm_w": 1.0 + nrm(ks[8], (DEPTH, A_HEAD_DIM), 0.02),
        "mlstm_norm_w": 1.0 + nrm(ks[9], (DEPTH, M_HEADS, M_V_DIM), 0.02),
        "attn_norm_w": 1.0 + nrm(ks[10], (DEPTH, A_HEADS, A_HEAD_DIM), 0.02),
        "w_out": nrm(ks[11], (DEPTH, MIX_WIDTH, D_MODEL), MIX_WIDTH ** -0.5),
        "norm2_w": 1.0 + nrm(ks[12], (DEPTH, D_MODEL), 0.02),
        "w_gate": nrm(ks[13], (DEPTH, D_MODEL, D_FF), D_MODEL ** -0.5),
        "w_up": nrm(ks[14], (DEPTH, D_MODEL, D_FF), D_MODEL ** -0.5),
        "w_down": nrm(ks[15], (DEPTH, D_FF, D_MODEL), D_FF ** -0.5),
    }


def reference(x, norm1_w, w_in, conv_w, conv_b, igate_b, fgate_b, q_norm_w, k_norm_w,
              mlstm_norm_w, attn_norm_w, w_out, norm2_w, w_gate, w_up, w_down):
    dtype = x.dtype
    s = x.shape[1]
    cos, sin = rope_tables(s, A_HEAD_DIM)
    split_points = np.cumsum(SPLIT_SIZES)[:-1].tolist()
    h = x
    for l in range(DEPTH):
        u = rms_norm(h, norm1_w[l]).astype(dtype)
        proj = jnp.matmul(u, w_in[l]).astype(jnp.float32)
        mq, mk, mv, mo, mi, mf, aq, ak, av = jnp.split(proj, split_points, axis=-1)

        qk = jax.nn.silu(causal_short_conv(jnp.concatenate([mq, mk], axis=-1),
                                           conv_w[l].astype(jnp.float32), conv_b[l].astype(jnp.float32)))
        mq, mk = qk[..., :M_QK_WIDTH], qk[..., M_QK_WIDTH:]
        i_pre = soft_cap(mi + igate_b[l].astype(jnp.float32)).transpose(0, 2, 1)
        f_pre = soft_cap(mf + fgate_b[l].astype(jnp.float32)).transpose(0, 2, 1)
        hm = mlstm_chunkwise(to_heads(mq, M_HEADS), to_heads(mk, M_HEADS), to_heads(mv, M_HEADS), i_pre, f_pre)
        hm = rms_norm(hm, mlstm_norm_w[l][:, None, :])
        hm = from_heads(hm) * jax.nn.sigmoid(mo)

        qa = apply_rope(rms_norm(to_heads(aq, A_HEADS), q_norm_w[l]), cos, sin)
        ka = apply_rope(rms_norm(to_heads(ak, A_HEADS), k_norm_w[l]), cos, sin)
        va = to_heads(av, A_HEADS)
        outs = []
        lses = []
        for window, dilation in ATTN_PATTERNS:
            o_p, lse_p = dilated_window_attention(qa, ka, va, window, dilation)
            outs.append(o_p)
            lses.append(lse_p)
        alpha = jax.nn.softmax(jnp.stack(lses, axis=0), axis=0)
        ha = jnp.einsum('pbhs,pbhsd->bhsd', alpha, jnp.stack(outs, axis=0))
        ha = from_heads(rms_norm(ha, attn_norm_w[l][:, None, :]))

        mix = jnp.concatenate([hm, ha], axis=-1).astype(dtype)
        h = h + jnp.matmul(mix, w_out[l])

        u2 = rms_norm(h, norm2_w[l]).astype(dtype)
        ff = jax.nn.silu(jnp.matmul(u2, w_gate[l])) * jnp.matmul(u2, w_up[l])
        h = h + jnp.matmul(ff, w_down[l])
    return h.astype(dtype)
```

```python
import functools

import jax
import jax.numpy as jnp
from jax import lax
from jax.experimental import pallas as pl
from jax.experimental.pallas import tpu as pltpu

D_MODEL = 2048
M_HEADS = 4
A_HEADS = 8
M_V_DIM = 256
M_QK_DIM = 128
A_HEAD_DIM = 128
CONV_WIDTH = 4
GATE_CAP = 15.0
ROPE_THETA = 10000.0
NORM_EPS = 1e-6
D_FF = 5632
M_QK_WIDTH = M_HEADS * M_QK_DIM
M_V_WIDTH = M_HEADS * M_V_DIM
A_WIDTH = A_HEADS * A_HEAD_DIM
PROJ_WIDTH = 2 * M_QK_WIDTH + 2 * M_V_WIDTH + 3 * A_WIDTH
GATE_WIDTH = 256

LANES = 128
SUBLANES = 8
VMEM_LIMIT = 56 * 1024 * 1024

MLSTM_CHUNK = 256
ATTN_WINDOW_SUB = 128
ATTN_DILATIONS = (1, 4, 16)
ATTN_RESIDUES = 16

BF16 = jnp.bfloat16
F32 = jnp.float32


def _sigmoid(x):
    return 1.0 / (1.0 + jnp.exp(-x))


def _rms(x, w):
    return x * lax.rsqrt(jnp.mean(x * x, axis=-1, keepdims=True) + NORM_EPS) * w


def _in_proj_kernel(x_ref, n1_ref, w_ref, wg_ref, o_ref, og_ref, u_scr):
    @pl.when(pl.program_id(1) == 0)
    def _():
        u = _rms(x_ref[...], n1_ref[...]).astype(BF16)
        u_scr[...] = u
        og_ref[...] = jnp.dot(u, wg_ref[...], preferred_element_type=F32)

    o_ref[...] = jnp.dot(u_scr[...], w_ref[...], preferred_element_type=F32)


def _in_proj(x2d, n1, w_main, w_gate, tm=1024, tn=1024):
    t = x2d.shape[0]
    return pl.pallas_call(
        _in_proj_kernel,
        grid=(t // tm, PROJ_WIDTH // tn),
        in_specs=[
            pl.BlockSpec((tm, D_MODEL), lambda i, j: (i, 0)),
            pl.BlockSpec((1, D_MODEL), lambda i, j: (0, 0)),
            pl.BlockSpec((D_MODEL, tn), lambda i, j: (0, j)),
            pl.BlockSpec((D_MODEL, GATE_WIDTH), lambda i, j: (0, 0)),
        ],
        out_specs=[
            pl.BlockSpec((tm, tn), lambda i, j: (i, j)),
            pl.BlockSpec((tm, GATE_WIDTH), lambda i, j: (i, 0)),
        ],
        out_shape=[
            jax.ShapeDtypeStruct((t, PROJ_WIDTH), F32),
            jax.ShapeDtypeStruct((t, GATE_WIDTH), F32),
        ],
        scratch_shapes=[pltpu.VMEM((tm, D_MODEL), BF16)],
        compiler_params=pltpu.CompilerParams(
            dimension_semantics=("parallel", "arbitrary"), vmem_limit_bytes=VMEM_LIMIT),
        name="in_proj",
    )(x2d, n1, w_main, w_gate)


def _mlstm_kernel(qk_ref, qkprev_ref, v_ref, o_ref, g_ref, cw_ref, cb_ref, gb_ref, nw_ref,
                  out_ref, xbuf, c_scr, m_scr):
    L = MLSTM_CHUNK
    c = pl.program_id(1)

    @pl.when(c == 0)
    def _():
        c_scr[...] = jnp.zeros_like(c_scr)
        m_scr[...] = jnp.zeros_like(m_scr)

    xbuf[0:SUBLANES, :] = jnp.where(c > 0, qkprev_ref[0], 0.0)
    xbuf[SUBLANES:, :] = qk_ref[0]
    y = cb_ref[...]
    for j in range(CONV_WIDTH):
        y = y + cw_ref[j:j + 1, :] * xbuf[pl.ds(SUBLANES - (CONV_WIDTH - 1) + j, L), :]
    qk = y * _sigmoid(y)

    gt = g_ref[0]
    i_pre = GATE_CAP * jnp.tanh((gt[:, :LANES] + gb_ref[:, :LANES]) / GATE_CAP)
    f_pre = GATE_CAP * jnp.tanh((gt[:, LANES:] + gb_ref[:, LANES:]) / GATE_CAP)
    log_f = jnp.minimum(f_pre, 0.0) - jnp.log1p(jnp.exp(-jnp.abs(f_pre)))

    row = lax.broadcasted_iota(jnp.int32, (L, L), 0)
    col = lax.broadcasted_iota(jnp.int32, (L, L), 1)
    causal = col <= row
    tril = jnp.where(causal, 1.0, 0.0).astype(BF16)
    hi = log_f.astype(BF16)
    r1 = log_f - hi.astype(F32)
    mid = r1.astype(BF16)
    lo = (r1 - mid.astype(F32)).astype(BF16)
    cum = (jnp.dot(tril, hi, preferred_element_type=F32)
           + jnp.dot(tril, mid, preferred_element_type=F32)
           + jnp.dot(tril, lo, preferred_element_type=F32))

    g_tot = cum[L - 1:L, :]
    a = g_tot - cum + i_pre
    m_loc = jnp.max(a, axis=0, keepdims=True)
    w_state = jnp.exp(a - m_loc)
    m_prev = m_scr[...]
    m_new = jnp.maximum(g_tot + m_prev, m_loc)
    s_old = jnp.exp(g_tot + m_prev - m_new)
    s_new = jnp.exp(m_loc - m_new)
    inter_log = cum + m_prev
    b_t = (i_pre - cum).T

    lane = lax.broadcasted_iota(jnp.int32, (L, LANES), 1)
    one_col = jnp.where(lane == 0, 1.0, 0.0).astype(BF16)
    vall = v_ref[0]
    oall = o_ref[0]
    for h in range(M_HEADS):
        q = (qk[:, h * M_QK_DIM:(h + 1) * M_QK_DIM] * (M_QK_DIM ** -0.5)).astype(BF16)
        kf = qk[:, M_QK_WIDTH + h * M_QK_DIM:M_QK_WIDTH + (h + 1) * M_QK_DIM]
        k = kf.astype(BF16)
        v_aug = jnp.concatenate(
            [vall[:, h * M_V_DIM:(h + 1) * M_V_DIM].astype(BF16), one_col], axis=1)

        d_log = jnp.where(causal, cum[:, h:h + 1] + b_t[h:h + 1, :], -jnp.inf)
        il = inter_log[:, h:h + 1]
        m_t = jnp.maximum(il, jnp.max(d_log, axis=1, keepdims=True))
        p = jnp.exp(d_log - m_t)
        w_inter = jnp.exp(il - m_t)
        s = lax.dot_general(q, k, (((1,), (1,)), ((), ())), preferred_element_type=F32)
        s_qk = (s * p).astype(BF16)

        c_prev = c_scr[h]
        tot = (w_inter * jnp.dot(q, c_prev.astype(BF16), preferred_element_type=F32)
               + jnp.dot(s_qk, v_aug, preferred_element_type=F32))
        num = tot[:, :M_V_DIM]
        den = tot[:, M_V_DIM:M_V_DIM + 1]
        hout = num / jnp.maximum(jnp.abs(den), jnp.exp(-m_t))
        hn = _rms(hout, nw_ref[:, h * M_V_DIM:(h + 1) * M_V_DIM])
        gate = _sigmoid(oall[:, h * M_V_DIM:(h + 1) * M_V_DIM])
        out_ref[0, :, h * M_V_DIM:(h + 1) * M_V_DIM] = (hn * gate).astype(BF16)

        kw_t = (kf * w_state[:, h:h + 1]).T.astype(BF16)
        kv = jnp.dot(kw_t, v_aug, preferred_element_type=F32)
        c_scr[h] = s_old[:, h:h + 1] * c_prev + s_new[:, h:h + 1] * kv
    m_scr[...] = m_new


def _mlstm(proj3, gates3, conv_w, conv_b, gate_b, norm_w):
    b, s, _ = proj3.shape
    L = MLSTM_CHUNK
    qk_w = 2 * M_QK_WIDTH
    return pl.pallas_call(
        _mlstm_kernel,
        grid=(b, s // L),
        in_specs=[
            pl.BlockSpec((1, L, qk_w), lambda i, c: (i, c, 0)),
            pl.BlockSpec((1, SUBLANES, qk_w),
                         lambda i, c: (i, jnp.maximum(c * (L // SUBLANES) - 1, 0), 0)),
            pl.BlockSpec((1, L, M_V_WIDTH), lambda i, c: (i, c, qk_w // M_V_WIDTH)),
            pl.BlockSpec((1, L, M_V_WIDTH), lambda i, c: (i, c, qk_w // M_V_WIDTH + 1)),
            pl.BlockSpec((1, L, GATE_WIDTH), lambda i, c: (i, c, 0)),
            pl.BlockSpec((CONV_WIDTH, qk_w), lambda i, c: (0, 0)),
            pl.BlockSpec((1, qk_w), lambda i, c: (0, 0)),
            pl.BlockSpec((1, GATE_WIDTH), lambda i, c: (0, 0)),
            pl.BlockSpec((1, M_V_WIDTH), lambda i, c: (0, 0)),
        ],
        out_specs=pl.BlockSpec((1, L, M_V_WIDTH), lambda i, c: (i, c, 0)),
        out_shape=jax.ShapeDtypeStruct((b, s, M_V_WIDTH), BF16),
        scratch_shapes=[
            pltpu.VMEM((L + SUBLANES, qk_w), F32),
            pltpu.VMEM((M_HEADS, M_QK_DIM, M_V_DIM + LANES), F32),
            pltpu.VMEM((1, LANES), F32),
        ],
        compiler_params=pltpu.CompilerParams(
            dimension_semantics=("parallel", "arbitrary"), vmem_limit_bytes=VMEM_LIMIT),
        name="mlstm",
    )(proj3, proj3, proj3, proj3, gates3, conv_w, conv_b, gate_b, norm_w)


def _attn_kernel(q_ref, k_ref, v_ref, cos_ref, sin_ref, qw_ref, kw_ref, aw_ref, out_ref,
                 qs, ks, vs, acc, ms, ls):
    s_len = q_ref.shape[1]
    R = ATTN_RESIDUES
    per = s_len // R
    W = ATTN_WINDOW_SUB

    def prep(r, carry):
        rows = pl.ds(r, per, stride=R)
        dst = pl.ds(pl.multiple_of(r * per, per), per)
        cos = cos_ref[rows, :]
        sin = sin_ref[rows, :]

        def rope(x, w):
            xn = _rms(x, w)
            return xn * cos + pltpu.roll(xn, A_HEAD_DIM // 2, 1) * sin

        qs[dst, :] = rope(q_ref[0, rows, :], qw_ref[...]) * (A_HEAD_DIM ** -0.5)
        ks[dst, :] = rope(k_ref[0, rows, :], kw_ref[...])
        vs[dst, :] = v_ref[0, rows, :]
        return carry

    lax.fori_loop(0, R, prep, 0)

    pos_q = lax.broadcasted_iota(jnp.int32, (W, 2 * W), 0)
    pos_k = lax.broadcasted_iota(jnp.int32, (W, 2 * W), 1)
    is_prev = pos_k < W
    prev_flag = jnp.where(is_prev, 1, 0)
    pos_k = jnp.where(is_prev, pos_k, pos_k - W)

    for pi, d in enumerate(ATTN_DILATIONS):
        run = SUBLANES * d
        nruns = W // run
        nb = per // run
        shift = run.bit_length() - 1

        def sub_pos(pos, run=run, nruns=nruns, shift=shift):
            return (pos & (run - 1)) * nruns + (pos >> shift)

        uq = sub_pos(pos_q)
        uk = sub_pos(pos_k)
        ahead = jnp.where(is_prev, uk - uq, uq - uk)
        bias = jnp.where(ahead >= 0, 0.0, -jnp.inf).astype(F32)

        def block(idx, carry, d=d, run=run, nruns=nruns, nb=nb, bias=bias, first=(pi == 0)):
            r = idx // nb
            jb = idx - r * nb
            starts = [pl.multiple_of((r + d * a) * per + run * jb, SUBLANES) for a in range(nruns)]
            pstarts = [jnp.maximum(st - run, 0) for st in starts]

            def gather(ref, sts):
                return jnp.concatenate([ref[pl.ds(st, run), :] for st in sts], axis=0)

            qb = gather(qs, starts).astype(BF16)
            kk = jnp.concatenate([gather(ks, pstarts), gather(ks, starts)], axis=0).astype(BF16)
            vv = jnp.concatenate([gather(vs, pstarts), gather(vs, starts)], axis=0).astype(BF16)
            s = lax.dot_general(qb, kk, (((1,), (1,)), ((), ())), preferred_element_type=F32)
            no_prev = prev_flag * jnp.where(jb == 0, 1, 0)
            s = jnp.where(no_prev > 0, -jnp.inf, s + bias)
            m_blk = jnp.max(s, axis=1, keepdims=True)
            p = jnp.exp(s - m_blk)
            l_blk = jnp.sum(p, axis=1, keepdims=True)
            o_blk = jnp.dot(p.astype(BF16), vv, preferred_element_type=F32)
            if first:
                a_new = o_blk
                m_new = jnp.broadcast_to(m_blk, (W, LANES))
                l_new = jnp.broadcast_to(l_blk, (W, LANES))
            else:
                m_old = gather(ms, starts)
                m_new = jnp.maximum(m_old, m_blk)
                e_old = jnp.exp(m_old - m_new)
                e_blk = jnp.exp(m_blk - m_new)
                a_new = gather(acc, starts) * e_old + o_blk * e_blk
                l_new = gather(ls, starts) * e_old + l_blk * e_blk
            for a, st in enumerate(starts):
                rows = pl.ds(st, run)
                acc[rows, :] = a_new[a * run:(a + 1) * run, :]
                ms[rows, :] = m_new[a * run:(a + 1) * run, :]
                ls[rows, :] = l_new[a * run:(a + 1) * run, :]
            return carry

        lax.fori_loop(0, d * nb, block, 0, unroll=2)

    def finish(r, carry):
        src = pl.ds(pl.multiple_of(r * per, per), per)
        o = acc[src, :] / ls[src, :]
        qs[pl.ds(r, per, stride=R), :] = _rms(o, aw_ref[...])
        return carry

    lax.fori_loop(0, R, finish, 0)
    out_ref[0] = qs[...].astype(BF16)


def _attention(proj3, cos_full, sin_signed, q_norm_w, k_norm_w, attn_norm_w):
    b, s, _ = proj3.shape
    base = (2 * M_QK_WIDTH + 2 * M_V_WIDTH) // A_HEAD_DIM
    nh = A_HEADS
    row_spec = lambda off: pl.BlockSpec((1, s, A_HEAD_DIM), lambda i, h: (i, 0, base + off + h))
    return pl.pallas_call(
        _attn_kernel,
        grid=(b, nh),
        in_specs=[
            row_spec(0), row_spec(nh), row_spec(2 * nh),
            pl.BlockSpec((s, A_HEAD_DIM), lambda i, h: (0, 0)),
            pl.BlockSpec((s, A_HEAD_DIM), lambda i, h: (0, 0)),
            pl.BlockSpec((1, A_HEAD_DIM), lambda i, h: (0, 0)),
            pl.BlockSpec((1, A_HEAD_DIM), lambda i, h: (0, 0)),
            pl.BlockSpec((1, A_HEAD_DIM), lambda i, h: (0, h)),
        ],
        out_specs=pl.BlockSpec((1, s, A_HEAD_DIM), lambda i, h: (i, 0, h)),
        out_shape=jax.ShapeDtypeStruct((b, s, A_WIDTH), BF16),
        scratch_shapes=[pltpu.VMEM((s, A_HEAD_DIM), F32) for _ in range(6)],
        compiler_params=pltpu.CompilerParams(
            dimension_semantics=("parallel", "parallel"), vmem_limit_bytes=VMEM_LIMIT),
        name="attention",
    )(proj3, proj3, proj3, cos_full, sin_signed, q_norm_w, k_norm_w, attn_norm_w)


def _out_proj_kernel(hm_ref, ha_ref, w_ref, x_ref, n2_ref, h_ref, u_ref):
    h1 = (x_ref[...]
          + jnp.dot(hm_ref[...], w_ref[:M_V_WIDTH, :], preferred_element_type=F32)
          + jnp.dot(ha_ref[...], w_ref[M_V_WIDTH:, :], preferred_element_type=F32))
    h_ref[...] = h1
    u_ref[...] = _rms(h1, n2_ref[...]).astype(BF16)


def _out_proj(hm, ha, w_out, x2d, n2, tm=512):
    t = x2d.shape[0]
    return pl.pallas_call(
        _out_proj_kernel,
        grid=(t // tm,),
        in_specs=[
            pl.BlockSpec((tm, M_V_WIDTH), lambda i: (i, 0)),
            pl.BlockSpec((tm, A_WIDTH), lambda i: (i, 0)),
            pl.BlockSpec((D_MODEL, D_MODEL), lambda i: (0, 0)),
            pl.BlockSpec((tm, D_MODEL), lambda i: (i, 0)),
            pl.BlockSpec((1, D_MODEL), lambda i: (0, 0)),
        ],
        out_specs=[
            pl.BlockSpec((tm, D_MODEL), lambda i: (i, 0)),
            pl.BlockSpec((tm, D_MODEL), lambda i: (i, 0)),
        ],
        out_shape=[
            jax.ShapeDtypeStruct((t, D_MODEL), F32),
            jax.ShapeDtypeStruct((t, D_MODEL), BF16),
        ],
        compiler_params=pltpu.CompilerParams(
            dimension_semantics=("parallel",), vmem_limit_bytes=VMEM_LIMIT),
        name="out_proj",
    )(hm, ha, w_out, x2d, n2)


def _ffn_kernel(u_ref, wg_ref, wu_ref, wd_ref, h_ref, o_ref):
    u = u_ref[...]
    g = jnp.dot(u, wg_ref[...], preferred_element_type=F32)
    up = jnp.dot(u, wu_ref[...], preferred_element_type=F32)
    act = (g * _sigmoid(g) * up).astype(BF16)
    part = jnp.dot(act, wd_ref[...], preferred_element_type=F32)

    @pl.when(pl.program_id(1) == 0)
    def _():
        o_ref[...] = h_ref[...] + part

    @pl.when(pl.program_id(1) > 0)
    def _():
        o_ref[...] += part


def _ffn(u2, wg, wu, wd, h1, tm=512, tf=512):
    t = u2.shape[0]
    return pl.pallas_call(
        _ffn_kernel,
        grid=(t // tm, D_FF // tf),
        in_specs=[
            pl.BlockSpec((tm, D_MODEL), lambda i, f: (i, 0)),
            pl.BlockSpec((D_MODEL, tf), lambda i, f: (0, f)),
            pl.BlockSpec((D_MODEL, tf), lambda i, f: (0, f)),
            pl.BlockSpec((tf, D_MODEL), lambda i, f: (f, 0)),
            pl.BlockSpec((tm, D_MODEL), lambda i, f: (i, 0)),
        ],
        out_specs=pl.BlockSpec((tm, D_MODEL), lambda i, f: (i, 0)),
        out_shape=jax.ShapeDtypeStruct((t, D_MODEL), F32),
        compiler_params=pltpu.CompilerParams(
            dimension_semantics=("parallel", "arbitrary"), vmem_limit_bytes=VMEM_LIMIT),
        name="ffn",
    )(u2, wg, wu, wd, h1)


def _rope_tables(s):
    inv_freq = ROPE_THETA ** (-jnp.arange(0, A_HEAD_DIM, 2, dtype=F32) / A_HEAD_DIM)
    ang = jnp.arange(s, dtype=F32)[:, None] * inv_freq[None, :]
    cos, sin = jnp.cos(ang), jnp.sin(ang)
    return jnp.concatenate([cos, cos], axis=1), jnp.concatenate([-sin, sin], axis=1)


def _pad_lanes(a, width):
    return jnp.pad(a, ((0, 0), (0, width - a.shape[1])))


def kernel(x, norm1_w, w_in, conv_w, conv_b, igate_b, fgate_b, q_norm_w, k_norm_w,
           mlstm_norm_w, attn_norm_w, w_out, norm2_w, w_gate, w_up, w_down):
    b, s, d = x.shape
    depth = norm1_w.shape[0]
    cos_full, sin_signed = _rope_tables(s)
    g0 = 2 * M_QK_WIDTH + 2 * M_V_WIDTH
    h = x
    for l in range(depth):
        w = w_in[l]
        w_main = jnp.concatenate([w[:, :g0], w[:, g0 + 2 * M_HEADS:]], axis=1).astype(BF16)
        w_gates = jnp.concatenate(
            [_pad_lanes(w[:, g0:g0 + M_HEADS], LANES),
             _pad_lanes(w[:, g0 + M_HEADS:g0 + 2 * M_HEADS], LANES)], axis=1).astype(BF16)
        gate_b = jnp.concatenate(
            [_pad_lanes(igate_b[l][None, :], LANES), _pad_lanes(fgate_b[l][None, :], LANES)], axis=1)

        x2d = h.reshape(b * s, d)
        proj, gates = _in_proj(x2d, norm1_w[l][None, :], w_main, w_gates)
        proj3 = proj.reshape(b, s, PROJ_WIDTH)
        gates3 = gates.reshape(b, s, GATE_WIDTH)

        hm = _mlstm(proj3, gates3, conv_w[l], conv_b[l][None, :], gate_b,
                    mlstm_norm_w[l].reshape(1, M_V_WIDTH))
        ha = _attention(proj3, cos_full, sin_signed, q_norm_w[l][None, :], k_norm_w[l][None, :],
                        attn_norm_w[l].reshape(1, A_WIDTH))

        h1, u2 = _out_proj(hm.reshape(b * s, M_V_WIDTH), ha.reshape(b * s, A_WIDTH),
                           w_out[l].astype(BF16), x2d, norm2_w[l][None, :])
        out = _ffn(u2, w_gate[l].astype(BF16), w_up[l].astype(BF16), w_down[l].astype(BF16), h1)
        h = out.reshape(b, s, d)
    return h
```

```python
import functools

import jax
import jax.numpy as jnp
from jax import lax
from jax.experimental import pallas as pl
from jax.experimental.pallas import tpu as pltpu

D_MODEL = 2048
M_HEADS = 4
A_HEADS = 8
M_V_DIM = 256
M_QK_DIM = 128
A_HEAD_DIM = 128
CONV_WIDTH = 4
GATE_CAP = 15.0
ROPE_THETA = 10000.0
NORM_EPS = 1e-6
D_FF = 5632
M_QK_WIDTH = M_HEADS * M_QK_DIM
M_V_WIDTH = M_HEADS * M_V_DIM
A_WIDTH = A_HEADS * A_HEAD_DIM
PROJ_WIDTH = 2 * M_QK_WIDTH + 2 * M_V_WIDTH + 3 * A_WIDTH
GATE_WIDTH = 256

LANES = 128
SUBLANES = 8
VMEM_LIMIT = 56 * 1024 * 1024

MLSTM_CHUNK = 256
ATTN_WINDOW_SUB = 128
ATTN_DILATIONS = (1, 4, 16)
ATTN_RESIDUES = 16
ATTN_GROUP = 8

BF16 = jnp.bfloat16
F32 = jnp.float32


def _sigmoid(x):
    return 1.0 / (1.0 + jnp.exp(-x))


def _rms(x, w):
    return x * lax.rsqrt(jnp.mean(x * x, axis=-1, keepdims=True) + NORM_EPS) * w


def _in_proj_kernel(x_ref, n1_ref, w_ref, wg_ref, o_ref, og_ref, u_scr):
    @pl.when(pl.program_id(1) == 0)
    def _():
        u = _rms(x_ref[...], n1_ref[...]).astype(BF16)
        u_scr[...] = u
        og_ref[...] = jnp.dot(u, wg_ref[...], preferred_element_type=F32)

    o_ref[...] = jnp.dot(u_scr[...], w_ref[...], preferred_element_type=F32)


def _in_proj(x2d, n1, w_main, w_gate, tm=1024, tn=1024):
    t = x2d.shape[0]
    return pl.pallas_call(
        _in_proj_kernel,
        grid=(t // tm, PROJ_WIDTH // tn),
        in_specs=[
            pl.BlockSpec((tm, D_MODEL), lambda i, j: (i, 0)),
            pl.BlockSpec((1, D_MODEL), lambda i, j: (0, 0)),
            pl.BlockSpec((D_MODEL, tn), lambda i, j: (0, j)),
            pl.BlockSpec((D_MODEL, GATE_WIDTH), lambda i, j: (0, 0)),
        ],
        out_specs=[
            pl.BlockSpec((tm, tn), lambda i, j: (i, j)),
            pl.BlockSpec((tm, GATE_WIDTH), lambda i, j: (i, 0)),
        ],
        out_shape=[
            jax.ShapeDtypeStruct((t, PROJ_WIDTH), F32),
            jax.ShapeDtypeStruct((t, GATE_WIDTH), F32),
        ],
        scratch_shapes=[pltpu.VMEM((tm, D_MODEL), BF16)],
        compiler_params=pltpu.CompilerParams(
            dimension_semantics=("parallel", "arbitrary"), vmem_limit_bytes=VMEM_LIMIT),
        name="in_proj",
    )(x2d, n1, w_main, w_gate)


def _mlstm_kernel(qk_ref, qkprev_ref, v_ref, o_ref, g_ref, cw_ref, cb_ref, gb_ref, nw_ref,
                  out_ref, xbuf, c_scr, m_scr):
    L = MLSTM_CHUNK
    c = pl.program_id(1)

    @pl.when(c == 0)
    def _():
        c_scr[...] = jnp.zeros_like(c_scr)
        m_scr[...] = jnp.zeros_like(m_scr)

    xbuf[0:SUBLANES, :] = jnp.where(c > 0, qkprev_ref[0], 0.0)
    xbuf[SUBLANES:, :] = qk_ref[0]
    y = cb_ref[...]
    for j in range(CONV_WIDTH):
        y = y + cw_ref[j:j + 1, :] * xbuf[pl.ds(SUBLANES - (CONV_WIDTH - 1) + j, L), :]
    qk = y * _sigmoid(y)

    gt = g_ref[0]
    i_pre = GATE_CAP * jnp.tanh((gt[:, :LANES] + gb_ref[:, :LANES]) / GATE_CAP)
    f_pre = GATE_CAP * jnp.tanh((gt[:, LANES:] + gb_ref[:, LANES:]) / GATE_CAP)
    log_f = jnp.minimum(f_pre, 0.0) - jnp.log1p(jnp.exp(-jnp.abs(f_pre)))

    row = lax.broadcasted_iota(jnp.int32, (L, L), 0)
    col = lax.broadcasted_iota(jnp.int32, (L, L), 1)
    causal = col <= row
    tril = jnp.where(causal, 1.0, 0.0).astype(BF16)
    hi = log_f.astype(BF16)
    r1 = log_f - hi.astype(F32)
    mid = r1.astype(BF16)
    lo = (r1 - mid.astype(F32)).astype(BF16)
    cum = (jnp.dot(tril, hi, preferred_element_type=F32)
           + jnp.dot(tril, mid, preferred_element_type=F32)
           + jnp.dot(tril, lo, preferred_element_type=F32))

    g_tot = cum[L - 1:L, :]
    a = g_tot - cum + i_pre
    m_loc = jnp.max(a, axis=0, keepdims=True)
    w_state = jnp.exp(a - m_loc)
    m_prev = m_scr[...]
    m_new = jnp.maximum(g_tot + m_prev, m_loc)
    s_old = jnp.exp(g_tot + m_prev - m_new)
    s_new = jnp.exp(m_loc - m_new)
    inter_log = cum + m_prev
    b_t = (i_pre - cum).T

    lane = lax.broadcasted_iota(jnp.int32, (L, LANES), 1)
    one_col = jnp.where(lane == 0, 1.0, 0.0).astype(BF16)
    vall = v_ref[0]
    oall = o_ref[0]
    for h in range(M_HEADS):
        q = (qk[:, h * M_QK_DIM:(h + 1) * M_QK_DIM] * (M_QK_DIM ** -0.5)).astype(BF16)
        kf = qk[:, M_QK_WIDTH + h * M_QK_DIM:M_QK_WIDTH + (h + 1) * M_QK_DIM]
        k = kf.astype(BF16)
        v_aug = jnp.concatenate(
            [vall[:, h * M_V_DIM:(h + 1) * M_V_DIM].astype(BF16), one_col], axis=1)

        d_log = jnp.where(causal, cum[:, h:h + 1] + b_t[h:h + 1, :], -jnp.inf)
        il = inter_log[:, h:h + 1]
        m_t = jnp.maximum(il, jnp.max(d_log, axis=1, keepdims=True))
        p = jnp.exp(d_log - m_t)
        w_inter = jnp.exp(il - m_t)
        s = lax.dot_general(q, k, (((1,), (1,)), ((), ())), preferred_element_type=F32)
        s_qk = (s * p).astype(BF16)

        c_prev = c_scr[h]
        tot = (w_inter * jnp.dot(q, c_prev.astype(BF16), preferred_element_type=F32)
               + jnp.dot(s_qk, v_aug, preferred_element_type=F32))
        num = tot[:, :M_V_DIM]
        den = tot[:, M_V_DIM:M_V_DIM + 1]
        hout = num / jnp.maximum(jnp.abs(den), jnp.exp(-m_t))
        hn = _rms(hout, nw_ref[:, h * M_V_DIM:(h + 1) * M_V_DIM])
        gate = _sigmoid(oall[:, h * M_V_DIM:(h + 1) * M_V_DIM])
        out_ref[0, :, h * M_V_DIM:(h + 1) * M_V_DIM] = (hn * gate).astype(BF16)

        kw_t = (kf * w_state[:, h:h + 1]).T.astype(BF16)
        kv = jnp.dot(kw_t, v_aug, preferred_element_type=F32)
        c_scr[h] = s_old[:, h:h + 1] * c_prev + s_new[:, h:h + 1] * kv
    m_scr[...] = m_new


def _mlstm(proj3, gates3, conv_w, conv_b, gate_b, norm_w):
    b, s, _ = proj3.shape
    L = MLSTM_CHUNK
    qk_w = 2 * M_QK_WIDTH
    return pl.pallas_call(
        _mlstm_kernel,
        grid=(b, s // L),
        in_specs=[
            pl.BlockSpec((1, L, qk_w), lambda i, c: (i, c, 0)),
            pl.BlockSpec((1, SUBLANES, qk_w),
                         lambda i, c: (i, jnp.maximum(c * (L // SUBLANES) - 1, 0), 0)),
            pl.BlockSpec((1, L, M_V_WIDTH), lambda i, c: (i, c, qk_w // M_V_WIDTH)),
            pl.BlockSpec((1, L, M_V_WIDTH), lambda i, c: (i, c, qk_w // M_V_WIDTH + 1)),
            pl.BlockSpec((1, L, GATE_WIDTH), lambda i, c: (i, c, 0)),
            pl.BlockSpec((CONV_WIDTH, qk_w), lambda i, c: (0, 0)),
            pl.BlockSpec((1, qk_w), lambda i, c: (0, 0)),
            pl.BlockSpec((1, GATE_WIDTH), lambda i, c: (0, 0)),
            pl.BlockSpec((1, M_V_WIDTH), lambda i, c: (0, 0)),
        ],
        out_specs=pl.BlockSpec((1, L, M_V_WIDTH), lambda i, c: (i, c, 0)),
        out_shape=jax.ShapeDtypeStruct((b, s, M_V_WIDTH), BF16),
        scratch_shapes=[
            pltpu.VMEM((L + SUBLANES, qk_w), F32),
            pltpu.VMEM((M_HEADS, M_QK_DIM, M_V_DIM + LANES), F32),
            pltpu.VMEM((1, LANES), F32),
        ],
        compiler_params=pltpu.CompilerParams(
            dimension_semantics=("parallel", "arbitrary"), vmem_limit_bytes=VMEM_LIMIT),
        name="mlstm",
    )(proj3, proj3, proj3, proj3, gates3, conv_w, conv_b, gate_b, norm_w)


def _attn_kernel(q_ref, k_ref, v_ref, cos_ref, sin_ref, qw_ref, kw_ref, aw_ref, out_ref,
                 qs, ks, vs, acc, ms, ls):
    s_len = q_ref.shape[1]
    R = ATTN_RESIDUES
    per = s_len // R
    W = ATTN_WINDOW_SUB

    ones = jnp.ones((A_HEAD_DIM, LANES), BF16)

    def prep(r, carry):
        rows = pl.ds(r, per, stride=R)
        dst = pl.ds(pl.multiple_of(r * per, per), per)
        cos = cos_ref[dst, :]
        sin = sin_ref[dst, :]

        def rope(x, w):
            sq = x * x
            hi = sq.astype(BF16)
            lo = (sq - hi.astype(F32)).astype(BF16)
            ssq = (jnp.dot(hi, ones, preferred_element_type=F32)
                   + jnp.dot(lo, ones, preferred_element_type=F32))
            xn = x * lax.rsqrt(ssq * (1.0 / A_HEAD_DIM) + NORM_EPS) * w
            return xn * cos + pltpu.roll(xn, A_HEAD_DIM // 2, 1) * sin

        qs[dst, :] = rope(q_ref[0, rows, :], qw_ref[...]) * (A_HEAD_DIM ** -0.5)
        ks[dst, :] = rope(k_ref[0, rows, :], kw_ref[...])
        vs[dst, :] = v_ref[0, rows, :]
        return carry

    lax.fori_loop(0, R, prep, 0, unroll=4)

    pos_q = lax.broadcasted_iota(jnp.int32, (W, W), 0)
    pos_k = lax.broadcasted_iota(jnp.int32, (W, W), 1)
    nt = (((1,), (1,)), ((), ()))
    G = ATTN_GROUP

    for pi, d in enumerate(ATTN_DILATIONS):
        run = SUBLANES * d
        nruns = W // run
        shift = run.bit_length() - 1

        def sub_pos(pos, run=run, nruns=nruns, shift=shift):
            return (pos & (run - 1)) * nruns + (pos >> shift)

        uq = sub_pos(pos_q)
        uk = sub_pos(pos_k)
        bias_cur = jnp.where(uk <= uq, 0.0, -jnp.inf).astype(F32)
        bias_prev = jnp.where(uk >= uq, 0.0, -jnp.inf).astype(F32)

        def group(gi, carry, d=d, run=run, nruns=nruns, bias_cur=bias_cur, bias_prev=bias_prev,
                  first=(pi == 0)):
            if d == 16:
                slab = [pl.multiple_of(gi * (G * W), G * W)]
            elif d == 4:
                slab = [pl.multiple_of(gi * per + a * (d * per), per) for a in range(nruns)]
            else:
                slab = [pl.multiple_of(a * per + gi * (G * run), G * run) for a in range(nruns)]

            def blocks(ref):
                slabs = [ref[pl.ds(st, G * run), :] for st in slab]
                return [jnp.concatenate([sl[g * run:(g + 1) * run, :] for sl in slabs], axis=0)
                        for g in range(G)]

            qg = [x.astype(BF16) for x in blocks(qs)]
            kg = [x.astype(BF16) for x in blocks(ks)]
            vg = [x.astype(BF16) for x in blocks(vs)]
            if d == 1:
                pst = [pl.multiple_of(a * per + jnp.maximum(gi * (G * run) - run, 0), run)
                       for a in range(nruns)]
                k_before = jnp.concatenate([ks[pl.ds(st, run), :] for st in pst], axis=0).astype(BF16)
                v_before = jnp.concatenate([vs[pl.ds(st, run), :] for st in pst], axis=0).astype(BF16)
                before_bias = jnp.where(gi > 0, 0.0, -jnp.inf)
            if not first:
                m_old_g, l_old_g, a_old_g = blocks(ms), blocks(ls), blocks(acc)

            if d == 16:
                has_prev = [g % 2 == 1 for g in range(G)]
            elif d == 4:
                has_prev = [g > 0 for g in range(G)]
            else:
                has_prev = [True] * G
            k_prev = [(k_before if d == 1 else None) if g == 0 else kg[g - 1] for g in range(G)]
            v_prev = [(v_before if d == 1 else None) if g == 0 else vg[g - 1] for g in range(G)]

            s_cur = [lax.dot_general(qg[g], kg[g], nt, preferred_element_type=F32) for g in range(G)]
            s_prv = [lax.dot_general(qg[g], k_prev[g], nt, preferred_element_type=F32)
                     if has_prev[g] else None for g in range(G)]
            p_cur, p_prv, m_blk, l_blk = [], [], [], []
            for g in range(G):
                s_c = s_cur[g] + bias_cur
                if has_prev[g]:
                    s_p = s_prv[g] + bias_prev
                    if g == 0:
                        s_p = s_p + before_bias
                    m = jnp.max(jnp.maximum(s_c, s_p), axis=1, keepdims=True)
                    p_c = jnp.exp(s_c - m)
                    p_p = jnp.exp(s_p - m)
                    l = jnp.sum(p_c + p_p, axis=1, keepdims=True)
                    p_prv.append(p_p.astype(BF16))
                else:
                    m = jnp.max(s_c, axis=1, keepdims=True)
                    p_c = jnp.exp(s_c - m)
                    l = jnp.sum(p_c, axis=1, keepdims=True)
                    p_prv.append(None)
                p_cur.append(p_c.astype(BF16))
                m_blk.append(m)
                l_blk.append(l)
            o_blk = []
            for g in range(G):
                o = jnp.dot(p_cur[g], vg[g], preferred_element_type=F32)
                if has_prev[g]:
                    o = o + jnp.dot(p_prv[g], v_prev[g], preferred_element_type=F32)
                o_blk.append(o)

            a_out, m_out, l_out = [], [], []
            for g in range(G):
                if first:
                    a_out.append(o_blk[g])
                    m_out.append(jnp.broadcast_to(m_blk[g], (W, LANES)))
                    l_out.append(jnp.broadcast_to(l_blk[g], (W, LANES)))
                else:
                    m_new = jnp.maximum(m_old_g[g], m_blk[g])
                    e_old = jnp.exp(m_old_g[g] - m_new)
                    e_blk = jnp.exp(m_blk[g] - m_new)
                    a_out.append(a_old_g[g] * e_old + o_blk[g] * e_blk)
                    l_out.append(l_old_g[g] * e_old + l_blk[g] * e_blk)
                    m_out.append(m_new)

            for ref, vals in ((acc, a_out), (ms, m_out), (ls, l_out)):
                for a, st in enumerate(slab):
                    ref[pl.ds(st, G * run), :] = jnp.concatenate(
                        [v[a * run:(a + 1) * run, :] for v in vals], axis=0)
            return carry

        lax.fori_loop(0, s_len // (G * W), group, 0)

    def finish(r, carry):
        src = pl.ds(pl.multiple_of(r * per, per), per)
        o = acc[src, :] / ls[src, :]
        qs[pl.ds(r, per, stride=R), :] = _rms(o, aw_ref[...])
        return carry

    lax.fori_loop(0, R, finish, 0, unroll=4)
    out_ref[0] = qs[...].astype(BF16)


def _attention(proj3, cos_full, sin_signed, q_norm_w, k_norm_w, attn_norm_w):
    b, s, _ = proj3.shape
    base = (2 * M_QK_WIDTH + 2 * M_V_WIDTH) // A_HEAD_DIM
    nh = A_HEADS
    row_spec = lambda off: pl.BlockSpec((1, s, A_HEAD_DIM), lambda i, h: (i, 0, base + off + h))
    return pl.pallas_call(
        _attn_kernel,
        grid=(b, nh),
        in_specs=[
            row_spec(0), row_spec(nh), row_spec(2 * nh),
            pl.BlockSpec((s, A_HEAD_DIM), lambda i, h: (0, 0)),
            pl.BlockSpec((s, A_HEAD_DIM), lambda i, h: (0, 0)),
            pl.BlockSpec((1, A_HEAD_DIM), lambda i, h: (0, 0)),
            pl.BlockSpec((1, A_HEAD_DIM), lambda i, h: (0, 0)),
            pl.BlockSpec((1, A_HEAD_DIM), lambda i, h: (0, h)),
        ],
        out_specs=pl.BlockSpec((1, s, A_HEAD_DIM), lambda i, h: (i, 0, h)),
        out_shape=jax.ShapeDtypeStruct((b, s, A_WIDTH), BF16),
        scratch_shapes=[pltpu.VMEM((s, A_HEAD_DIM), F32) for _ in range(6)],
        compiler_params=pltpu.CompilerParams(
            dimension_semantics=("parallel", "parallel"), vmem_limit_bytes=VMEM_LIMIT),
        name="attention",
    )(proj3, proj3, proj3, cos_full, sin_signed, q_norm_w, k_norm_w, attn_norm_w)


def _out_proj_kernel(hm_ref, ha_ref, w_ref, x_ref, n2_ref, h_ref, u_ref):
    h1 = (x_ref[...]
          + jnp.dot(hm_ref[...], w_ref[:M_V_WIDTH, :], preferred_element_type=F32)
          + jnp.dot(ha_ref[...], w_ref[M_V_WIDTH:, :], preferred_element_type=F32))
    h_ref[...] = h1
    u_ref[...] = _rms(h1, n2_ref[...]).astype(BF16)


def _out_proj(hm, ha, w_out, x2d, n2, tm=512):
    t = x2d.shape[0]
    return pl.pallas_call(
        _out_proj_kernel,
        grid=(t // tm,),
        in_specs=[
            pl.BlockSpec((tm, M_V_WIDTH), lambda i: (i, 0)),
            pl.BlockSpec((tm, A_WIDTH), lambda i: (i, 0)),
            pl.BlockSpec((D_MODEL, D_MODEL), lambda i: (0, 0)),
            pl.BlockSpec((tm, D_MODEL), lambda i: (i, 0)),
            pl.BlockSpec((1, D_MODEL), lambda i: (0, 0)),
        ],
        out_specs=[
            pl.BlockSpec((tm, D_MODEL), lambda i: (i, 0)),
            pl.BlockSpec((tm, D_MODEL), lambda i: (i, 0)),
        ],
        out_shape=[
            jax.ShapeDtypeStruct((t, D_MODEL), F32),
            jax.ShapeDtypeStruct((t, D_MODEL), BF16),
        ],
        compiler_params=pltpu.CompilerParams(
            dimension_semantics=("parallel",), vmem_limit_bytes=VMEM_LIMIT),
        name="out_proj",
    )(hm, ha, w_out, x2d, n2)


def _ffn_kernel(u_ref, wg_ref, wu_ref, wd_ref, h_ref, o_ref):
    u = u_ref[...]
    g = jnp.dot(u, wg_ref[...], preferred_element_type=F32)
    up = jnp.dot(u, wu_ref[...], preferred_element_type=F32)
    act = (g * _sigmoid(g) * up).astype(BF16)
    part = jnp.dot(act, wd_ref[...], preferred_element_type=F32)

    @pl.when(pl.program_id(1) == 0)
    def _():
        o_ref[...] = h_ref[...] + part

    @pl.when(pl.program_id(1) > 0)
    def _():
        o_ref[...] += part


def _ffn(u2, wg, wu, wd, h1, tm=512, tf=512):
    t = u2.shape[0]
    return pl.pallas_call(
        _ffn_kernel,
        grid=(t // tm, D_FF // tf),
        in_specs=[
            pl.BlockSpec((tm, D_MODEL), lambda i, f: (i, 0)),
            pl.BlockSpec((D_MODEL, tf), lambda i, f: (0, f)),
            pl.BlockSpec((D_MODEL, tf), lambda i, f: (0, f)),
            pl.BlockSpec((tf, D_MODEL), lambda i, f: (f, 0)),
            pl.BlockSpec((tm, D_MODEL), lambda i, f: (i, 0)),
        ],
        out_specs=pl.BlockSpec((tm, D_MODEL), lambda i, f: (i, 0)),
        out_shape=jax.ShapeDtypeStruct((t, D_MODEL), F32),
        compiler_params=pltpu.CompilerParams(
            dimension_semantics=("parallel", "arbitrary"), vmem_limit_bytes=VMEM_LIMIT),
        name="ffn",
    )(u2, wg, wu, wd, h1)


def _rope_tables(s):
    inv_freq = ROPE_THETA ** (-jnp.arange(0, A_HEAD_DIM, 2, dtype=F32) / A_HEAD_DIM)
    ang = jnp.arange(s, dtype=F32)[:, None] * inv_freq[None, :]
    cos, sin = jnp.cos(ang), jnp.sin(ang)

    def grouped(t):
        return t.reshape(s // ATTN_RESIDUES, ATTN_RESIDUES, A_HEAD_DIM).transpose(1, 0, 2).reshape(
            s, A_HEAD_DIM)

    return (grouped(jnp.concatenate([cos, cos], axis=1)),
            grouped(jnp.concatenate([-sin, sin], axis=1)))


def _pad_lanes(a, width):
    return jnp.pad(a, ((0, 0), (0, width - a.shape[1])))


def kernel(x, norm1_w, w_in, conv_w, conv_b, igate_b, fgate_b, q_norm_w, k_norm_w,
           mlstm_norm_w, attn_norm_w, w_out, norm2_w, w_gate, w_up, w_down):
    b, s, d = x.shape
    depth = norm1_w.shape[0]
    cos_full, sin_signed = _rope_tables(s)
    g0 = 2 * M_QK_WIDTH + 2 * M_V_WIDTH
    h = x
    for l in range(depth):
        w = w_in[l]
        w_main = jnp.concatenate([w[:, :g0], w[:, g0 + 2 * M_HEADS:]], axis=1).astype(BF16)
        w_gates = jnp.concatenate(
            [_pad_lanes(w[:, g0:g0 + M_HEADS], LANES),
             _pad_lanes(w[:, g0 + M_HEADS:g0 + 2 * M_HEADS], LANES)], axis=1).astype(BF16)
        gate_b = jnp.concatenate(
            [_pad_lanes(igate_b[l][None, :], LANES), _pad_lanes(fgate_b[l][None, :], LANES)], axis=1)

        x2d = h.reshape(b * s, d)
        proj, gates = _in_proj(x2d, norm1_w[l][None, :], w_main, w_gates)
        proj3 = proj.reshape(b, s, PROJ_WIDTH)
        gates3 = gates.reshape(b, s, GATE_WIDTH)

        hm = _mlstm(proj3, gates3, conv_w[l], conv_b[l][None, :], gate_b,
                    mlstm_norm_w[l].reshape(1, M_V_WIDTH))
        ha = _attention(proj3, cos_full, sin_signed, q_norm_w[l][None, :], k_norm_w[l][None, :],
                        attn_norm_w[l].reshape(1, A_WIDTH))

        h1, u2 = _out_proj(hm.reshape(b * s, M_V_WIDTH), ha.reshape(b * s, A_WIDTH),
                           w_out[l].astype(BF16), x2d, norm2_w[l][None, :])
        out = _ffn(u2, w_gate[l].astype(BF16), w_up[l].astype(BF16), w_down[l].astype(BF16), h1)
        h = out.reshape(b, s, d)
    return h
```

```python
import numpy as np
import jax
import jax.numpy as jnp
from jax import lax
from jax.experimental import pallas as pl
from jax.experimental.pallas import tpu as pltpu

D_MODEL = 2048
M_HEADS = 4
A_HEADS = 8
M_V_DIM = 256
M_QK_DIM = 128
A_HEAD_DIM = 128
CONV_WIDTH = 4
GATE_CAP = 15.0
ROPE_THETA = 10000.0
NORM_EPS = 1e-6
D_FF = 5632
M_QK_WIDTH = M_HEADS * M_QK_DIM
M_V_WIDTH = M_HEADS * M_V_DIM
A_WIDTH = A_HEADS * A_HEAD_DIM
PROJ_WIDTH = 2 * M_QK_WIDTH + 2 * M_V_WIDTH + 3 * A_WIDTH
GATE_WIDTH = 256

LANES = 128
SUBLANES = 8
VMEM_LIMIT = 56 * 1024 * 1024

MLSTM_CHUNK = 256
ATTN_WINDOW_SUB = 128
ATTN_DILATIONS = (1, 4, 16)
ATTN_RESIDUES = 16
ATTN_GROUP = 8

BF16 = jnp.bfloat16
F32 = jnp.float32


def _sigmoid(x):
    return 1.0 / (1.0 + jnp.exp(-x))


def _rms(x, w):
    return x * lax.rsqrt(jnp.mean(x * x, axis=-1, keepdims=True) + NORM_EPS) * w


def _w_in_prep_kernel(w_ref, main_ref, gate_ref):
    g0 = 2 * M_QK_WIDTH + 2 * M_V_WIDTH
    w = w_ref[...]
    main_ref[:, :g0] = w[:, :g0].astype(BF16)
    main_ref[:, g0:] = w[:, g0 + 2 * M_HEADS:].astype(BF16)
    gblk = w[:, g0:g0 + LANES]
    lane = lax.broadcasted_iota(jnp.int32, gblk.shape, 1)
    gate_ref[:, :LANES] = jnp.where(lane < M_HEADS, gblk, 0.0).astype(BF16)
    gate_ref[:, LANES:] = jnp.where(
        lane < M_HEADS, pltpu.roll(gblk, LANES - M_HEADS, 1), 0.0).astype(BF16)


def _w_in_prep(w, tr=256):
    k, n = w.shape
    return pl.pallas_call(
        _w_in_prep_kernel,
        grid=(k // tr,),
        in_specs=[pl.BlockSpec((tr, n), lambda i: (i, 0))],
        out_specs=[
            pl.BlockSpec((tr, PROJ_WIDTH), lambda i: (i, 0)),
            pl.BlockSpec((tr, GATE_WIDTH), lambda i: (i, 0)),
        ],
        out_shape=[
            jax.ShapeDtypeStruct((k, PROJ_WIDTH), BF16),
            jax.ShapeDtypeStruct((k, GATE_WIDTH), BF16),
        ],
        compiler_params=pltpu.CompilerParams(
            dimension_semantics=("parallel",), vmem_limit_bytes=VMEM_LIMIT),
        name="w_in_prep",
    )(w)


def _in_proj_kernel(x_ref, n1_ref, w_ref, wg_ref, o_ref, og_ref, u_scr):
    @pl.when(pl.program_id(1) == 0)
    def _():
        u = _rms(x_ref[...], n1_ref[...]).astype(BF16)
        u_scr[...] = u
        og_ref[...] = jnp.dot(u, wg_ref[...], preferred_element_type=F32)

    o_ref[...] = jnp.dot(u_scr[...], w_ref[...], preferred_element_type=F32)


def _in_proj(x2d, n1, w_main, w_gate, tm=1024, tn=1024):
    t = x2d.shape[0]
    return pl.pallas_call(
        _in_proj_kernel,
        grid=(t // tm, PROJ_WIDTH // tn),
        in_specs=[
            pl.BlockSpec((tm, D_MODEL), lambda i, j: (i, 0)),
            pl.BlockSpec((1, D_MODEL), lambda i, j: (0, 0)),
            pl.BlockSpec((D_MODEL, tn), lambda i, j: (0, j)),
            pl.BlockSpec((D_MODEL, GATE_WIDTH), lambda i, j: (0, 0)),
        ],
        out_specs=[
            pl.BlockSpec((tm, tn), lambda i, j: (i, j)),
            pl.BlockSpec((tm, GATE_WIDTH), lambda i, j: (i, 0)),
        ],
        out_shape=[
            jax.ShapeDtypeStruct((t, PROJ_WIDTH), F32),
            jax.ShapeDtypeStruct((t, GATE_WIDTH), F32),
        ],
        scratch_shapes=[pltpu.VMEM((tm, D_MODEL), BF16)],
        compiler_params=pltpu.CompilerParams(
            dimension_semantics=("parallel", "arbitrary"), vmem_limit_bytes=VMEM_LIMIT),
        name="in_proj",
    )(x2d, n1, w_main, w_gate)


def _mlstm_kernel(qk_ref, qkprev_ref, v_ref, o_ref, g_ref, cw_ref, cb_ref, gb_ref, nw_ref,
                  out_ref, xbuf, c_scr, m_scr):
    L = MLSTM_CHUNK
    c = pl.program_id(1)

    @pl.when(c == 0)
    def _():
        c_scr[...] = jnp.zeros_like(c_scr)
        m_scr[...] = jnp.zeros_like(m_scr)

    xbuf[0:SUBLANES, :] = jnp.where(c > 0, qkprev_ref[0], 0.0)
    xbuf[SUBLANES:, :] = qk_ref[0]
    y = cb_ref[...]
    for j in range(CONV_WIDTH):
        y = y + cw_ref[j:j + 1, :] * xbuf[pl.ds(SUBLANES - (CONV_WIDTH - 1) + j, L), :]
    qk = y * _sigmoid(y)

    gt = g_ref[0]
    i_pre = GATE_CAP * jnp.tanh((gt[:, :LANES] + gb_ref[:, :LANES]) / GATE_CAP)
    f_pre = GATE_CAP * jnp.tanh((gt[:, LANES:] + gb_ref[:, LANES:]) / GATE_CAP)
    log_f = jnp.minimum(f_pre, 0.0) - jnp.log1p(jnp.exp(-jnp.abs(f_pre)))

    row = lax.broadcasted_iota(jnp.int32, (L, L), 0)
    col = lax.broadcasted_iota(jnp.int32, (L, L), 1)
    causal = col <= row
    tril = jnp.where(causal, 1.0, 0.0).astype(BF16)
    hi = log_f.astype(BF16)
    r1 = log_f - hi.astype(F32)
    mid = r1.astype(BF16)
    lo = (r1 - mid.astype(F32)).astype(BF16)
    cum = (jnp.dot(tril, hi, preferred_element_type=F32)
           + jnp.dot(tril, mid, preferred_element_type=F32)
           + jnp.dot(tril, lo, preferred_element_type=F32))

    g_tot = cum[L - 1:L, :]
    a = g_tot - cum + i_pre
    m_loc = jnp.max(a, axis=0, keepdims=True)
    w_state = jnp.exp(a - m_loc)
    m_prev = m_scr[...]
    m_new = jnp.maximum(g_tot + m_prev, m_loc)
    s_old = jnp.exp(g_tot + m_prev - m_new)
    s_new = jnp.exp(m_loc - m_new)
    inter_log = cum + m_prev
    b_t = (i_pre - cum).T

    lane = lax.broadcasted_iota(jnp.int32, (L, LANES), 1)
    one_col = jnp.where(lane == 0, 1.0, 0.0).astype(BF16)
    vall = v_ref[0]
    oall = o_ref[0]
    for h in range(M_HEADS):
        q = (qk[:, h * M_QK_DIM:(h + 1) * M_QK_DIM] * (M_QK_DIM ** -0.5)).astype(BF16)
        kf = qk[:, M_QK_WIDTH + h * M_QK_DIM:M_QK_WIDTH + (h + 1) * M_QK_DIM]
        k = kf.astype(BF16)
        v_aug = jnp.concatenate(
            [vall[:, h * M_V_DIM:(h + 1) * M_V_DIM].astype(BF16), one_col], axis=1)

        d_log = jnp.where(causal, cum[:, h:h + 1] + b_t[h:h + 1, :], -jnp.inf)
        il = inter_log[:, h:h + 1]
        m_t = jnp.maximum(il, jnp.max(d_log, axis=1, keepdims=True))
        p = jnp.exp(d_log - m_t)
        w_inter = jnp.exp(il - m_t)
        s = lax.dot_general(q, k, (((1,), (1,)), ((), ())), preferred_element_type=F32)
        s_qk = (s * p).astype(BF16)

        c_prev = c_scr[h]
        tot = (w_inter * jnp.dot(q, c_prev.astype(BF16), preferred_element_type=F32)
               + jnp.dot(s_qk, v_aug, preferred_element_type=F32))
        num = tot[:, :M_V_DIM]
        den = tot[:, M_V_DIM:M_V_DIM + 1]
        hout = num / jnp.maximum(jnp.abs(den), jnp.exp(-m_t))
        hn = _rms(hout, nw_ref[:, h * M_V_DIM:(h + 1) * M_V_DIM])
        gate = _sigmoid(oall[:, h * M_V_DIM:(h + 1) * M_V_DIM])
        out_ref[0, :, h * M_V_DIM:(h + 1) * M_V_DIM] = (hn * gate).astype(BF16)

        kw_t = (kf * w_state[:, h:h + 1]).T.astype(BF16)
        kv = jnp.dot(kw_t, v_aug, preferred_element_type=F32)
        c_scr[h] = s_old[:, h:h + 1] * c_prev + s_new[:, h:h + 1] * kv
    m_scr[...] = m_new


def _mlstm(proj3, gates3, conv_w, conv_b, gate_b, norm_w):
    b, s, _ = proj3.shape
    L = MLSTM_CHUNK
    qk_w = 2 * M_QK_WIDTH
    return pl.pallas_call(
        _mlstm_kernel,
        grid=(b, s // L),
        in_specs=[
            pl.BlockSpec((1, L, qk_w), lambda i, c: (i, c, 0)),
            pl.BlockSpec((1, SUBLANES, qk_w),
                         lambda i, c: (i, jnp.maximum(c * (L // SUBLANES) - 1, 0), 0)),
            pl.BlockSpec((1, L, M_V_WIDTH), lambda i, c: (i, c, qk_w // M_V_WIDTH)),
            pl.BlockSpec((1, L, M_V_WIDTH), lambda i, c: (i, c, qk_w // M_V_WIDTH + 1)),
            pl.BlockSpec((1, L, GATE_WIDTH), lambda i, c: (i, c, 0)),
            pl.BlockSpec((CONV_WIDTH, qk_w), lambda i, c: (0, 0)),
            pl.BlockSpec((1, qk_w), lambda i, c: (0, 0)),
            pl.BlockSpec((1, GATE_WIDTH), lambda i, c: (0, 0)),
            pl.BlockSpec((1, M_V_WIDTH), lambda i, c: (0, 0)),
        ],
        out_specs=pl.BlockSpec((1, L, M_V_WIDTH), lambda i, c: (i, c, 0)),
        out_shape=jax.ShapeDtypeStruct((b, s, M_V_WIDTH), BF16),
        scratch_shapes=[
            pltpu.VMEM((L + SUBLANES, qk_w), F32),
            pltpu.VMEM((M_HEADS, M_QK_DIM, M_V_DIM + LANES), F32),
            pltpu.VMEM((1, LANES), F32),
        ],
        compiler_params=pltpu.CompilerParams(
            dimension_semantics=("parallel", "arbitrary"), vmem_limit_bytes=VMEM_LIMIT),
        name="mlstm",
    )(proj3, proj3, proj3, proj3, gates3, conv_w, conv_b, gate_b, norm_w)


def _attn_kernel(q_ref, k_ref, v_ref, cos_ref, sin_ref, qw_ref, kw_ref, aw_ref, out_ref,
                 qs, ks, vs, acc, ms, ls):
    s_len = q_ref.shape[1]
    R = ATTN_RESIDUES
    per = s_len // R
    W = ATTN_WINDOW_SUB

    ones = jnp.ones((A_HEAD_DIM, LANES), BF16)

    def prep(r, carry):
        rows = pl.ds(r, per, stride=R)
        dst = pl.ds(pl.multiple_of(r * per, per), per)
        cos = cos_ref[dst, :]
        sin = sin_ref[dst, :]

        def rope(x, w):
            sq = x * x
            hi = sq.astype(BF16)
            lo = (sq - hi.astype(F32)).astype(BF16)
            ssq = (jnp.dot(hi, ones, preferred_element_type=F32)
                   + jnp.dot(lo, ones, preferred_element_type=F32))
            xn = x * lax.rsqrt(ssq * (1.0 / A_HEAD_DIM) + NORM_EPS) * w
            return xn * cos + pltpu.roll(xn, A_HEAD_DIM // 2, 1) * sin

        qs[dst, :] = rope(q_ref[0, rows, :], qw_ref[...]) * (A_HEAD_DIM ** -0.5)
        ks[dst, :] = rope(k_ref[0, rows, :], kw_ref[...])
        vs[dst, :] = v_ref[0, rows, :]
        return carry

    lax.fori_loop(0, R, prep, 0, unroll=4)

    pos_q = lax.broadcasted_iota(jnp.int32, (W, W), 0)
    pos_k = lax.broadcasted_iota(jnp.int32, (W, W), 1)
    nt = (((1,), (1,)), ((), ()))
    G = ATTN_GROUP

    for pi, d in enumerate(ATTN_DILATIONS):
        run = SUBLANES * d
        nruns = W // run
        shift = run.bit_length() - 1

        def sub_pos(pos, run=run, nruns=nruns, shift=shift):
            return (pos & (run - 1)) * nruns + (pos >> shift)

        uq = sub_pos(pos_q)
        uk = sub_pos(pos_k)
        bias_cur = jnp.where(uk <= uq, 0.0, -jnp.inf).astype(F32)
        bias_prev = jnp.where(uk >= uq, 0.0, -jnp.inf).astype(F32)

        def group(gi, carry, d=d, run=run, nruns=nruns, bias_cur=bias_cur, bias_prev=bias_prev,
                  first=(pi == 0)):
            if d == 16:
                slab = [pl.multiple_of(gi * (G * W), G * W)]
            elif d == 4:
                slab = [pl.multiple_of(gi * per + a * (d * per), per) for a in range(nruns)]
            else:
                slab = [pl.multiple_of(a * per + gi * (G * run), G * run) for a in range(nruns)]

            def blocks(ref):
                slabs = [ref[pl.ds(st, G * run), :] for st in slab]
                return [jnp.concatenate([sl[g * run:(g + 1) * run, :] for sl in slabs], axis=0)
                        for g in range(G)]

            qg = [x.astype(BF16) for x in blocks(qs)]
            kg = [x.astype(BF16) for x in blocks(ks)]
            vg = [x.astype(BF16) for x in blocks(vs)]
            if d == 1:
                pst = [pl.multiple_of(a * per + jnp.maximum(gi * (G * run) - run, 0), run)
                       for a in range(nruns)]
                k_before = jnp.concatenate([ks[pl.ds(st, run), :] for st in pst], axis=0).astype(BF16)
                v_before = jnp.concatenate([vs[pl.ds(st, run), :] for st in pst], axis=0).astype(BF16)
                before_bias = jnp.where(gi > 0, 0.0, -jnp.inf)
            if not first:
                m_old_g, l_old_g, a_old_g = blocks(ms), blocks(ls), blocks(acc)

            if d == 16:
                has_prev = [g % 2 == 1 for g in range(G)]
            elif d == 4:
                has_prev = [g > 0 for g in range(G)]
            else:
                has_prev = [True] * G
            k_prev = [(k_before if d == 1 else None) if g == 0 else kg[g - 1] for g in range(G)]
            v_prev = [(v_before if d == 1 else None) if g == 0 else vg[g - 1] for g in range(G)]

            s_cur = [lax.dot_general(qg[g], kg[g], nt, preferred_element_type=F32) for g in range(G)]
            s_prv = [lax.dot_general(qg[g], k_prev[g], nt, preferred_element_type=F32)
                     if has_prev[g] else None for g in range(G)]
            p_cur, p_prv, m_blk, l_blk = [], [], [], []
            for g in range(G):
                s_c = s_cur[g] + bias_cur
                if has_prev[g]:
                    s_p = s_prv[g] + bias_prev
                    if g == 0:
                        s_p = s_p + before_bias
                    m = jnp.max(jnp.maximum(s_c, s_p), axis=1, keepdims=True)
                    p_c = jnp.exp(s_c - m)
                    p_p = jnp.exp(s_p - m)
                    l = jnp.sum(p_c + p_p, axis=1, keepdims=True)
                    p_prv.append(p_p.astype(BF16))
                else:
                    m = jnp.max(s_c, axis=1, keepdims=True)
                    p_c = jnp.exp(s_c - m)
                    l = jnp.sum(p_c, axis=1, keepdims=True)
                    p_prv.append(None)
                p_cur.append(p_c.astype(BF16))
                m_blk.append(m)
                l_blk.append(l)
            o_blk = []
            for g in range(G):
                o = jnp.dot(p_cur[g], vg[g], preferred_element_type=F32)
                if has_prev[g]:
                    o = o + jnp.dot(p_prv[g], v_prev[g], preferred_element_type=F32)
                o_blk.append(o)

            a_out, m_out, l_out = [], [], []
            for g in range(G):
                if first:
                    a_out.append(o_blk[g])
                    m_out.append(jnp.broadcast_to(m_blk[g], (W, LANES)))
                    l_out.append(jnp.broadcast_to(l_blk[g], (W, LANES)))
                else:
                    m_new = jnp.maximum(m_old_g[g], m_blk[g])
                    e_old = jnp.exp(m_old_g[g] - m_new)
                    e_blk = jnp.exp(m_blk[g] - m_new)
                    a_out.append(a_old_g[g] * e_old + o_blk[g] * e_blk)
                    l_out.append(l_old_g[g] * e_old + l_blk[g] * e_blk)
                    m_out.append(m_new)

            for ref, vals in ((acc, a_out), (ms, m_out), (ls, l_out)):
                for a, st in enumerate(slab):
                    ref[pl.ds(st, G * run), :] = jnp.concatenate(
                        [v[a * run:(a + 1) * run, :] for v in vals], axis=0)
            return carry

        lax.fori_loop(0, s_len // (G * W), group, 0)

    def finish(r, carry):
        src = pl.ds(pl.multiple_of(r * per, per), per)
        o = acc[src, :] / ls[src, :]
        qs[pl.ds(r, per, stride=R), :] = _rms(o, aw_ref[...])
        return carry

    lax.fori_loop(0, R, finish, 0, unroll=4)
    out_ref[0] = qs[...].astype(BF16)


def _attention(proj3, cos_full, sin_signed, q_norm_w, k_norm_w, attn_norm_w):
    b, s, _ = proj3.shape
    base = (2 * M_QK_WIDTH + 2 * M_V_WIDTH) // A_HEAD_DIM
    nh = A_HEADS
    row_spec = lambda off: pl.BlockSpec((1, s, A_HEAD_DIM), lambda i, h: (i, 0, base + off + h))
    return pl.pallas_call(
        _attn_kernel,
        grid=(b, nh),
        in_specs=[
            row_spec(0), row_spec(nh), row_spec(2 * nh),
            pl.BlockSpec((s, A_HEAD_DIM), lambda i, h: (0, 0)),
            pl.BlockSpec((s, A_HEAD_DIM), lambda i, h: (0, 0)),
            pl.BlockSpec((1, A_HEAD_DIM), lambda i, h: (0, 0)),
            pl.BlockSpec((1, A_HEAD_DIM), lambda i, h: (0, 0)),
            pl.BlockSpec((1, A_HEAD_DIM), lambda i, h: (0, h)),
        ],
        out_specs=pl.BlockSpec((1, s, A_HEAD_DIM), lambda i, h: (i, 0, h)),
        out_shape=jax.ShapeDtypeStruct((b, s, A_WIDTH), BF16),
        scratch_shapes=[pltpu.VMEM((s, A_HEAD_DIM), F32) for _ in range(6)],
        compiler_params=pltpu.CompilerParams(
            dimension_semantics=("parallel", "parallel"), vmem_limit_bytes=VMEM_LIMIT),
        name="attention",
    )(proj3, proj3, proj3, cos_full, sin_signed, q_norm_w, k_norm_w, attn_norm_w)


def _out_proj_kernel(hm_ref, ha_ref, w_ref, x_ref, n2_ref, h_ref, u_ref):
    h1 = (x_ref[...]
          + jnp.dot(hm_ref[...], w_ref[:M_V_WIDTH, :], preferred_element_type=F32)
          + jnp.dot(ha_ref[...], w_ref[M_V_WIDTH:, :], preferred_element_type=F32))
    h_ref[...] = h1
    u_ref[...] = _rms(h1, n2_ref[...]).astype(BF16)


def _out_proj(hm, ha, w_out, x2d, n2, tm=512):
    t = x2d.shape[0]
    return pl.pallas_call(
        _out_proj_kernel,
        grid=(t // tm,),
        in_specs=[
            pl.BlockSpec((tm, M_V_WIDTH), lambda i: (i, 0)),
            pl.BlockSpec((tm, A_WIDTH), lambda i: (i, 0)),
            pl.BlockSpec((D_MODEL, D_MODEL), lambda i: (0, 0)),
            pl.BlockSpec((tm, D_MODEL), lambda i: (i, 0)),
            pl.BlockSpec((1, D_MODEL), lambda i: (0, 0)),
        ],
        out_specs=[
            pl.BlockSpec((tm, D_MODEL), lambda i: (i, 0)),
            pl.BlockSpec((tm, D_MODEL), lambda i: (i, 0)),
        ],
        out_shape=[
            jax.ShapeDtypeStruct((t, D_MODEL), F32),
            jax.ShapeDtypeStruct((t, D_MODEL), BF16),
        ],
        compiler_params=pltpu.CompilerParams(
            dimension_semantics=("parallel",), vmem_limit_bytes=VMEM_LIMIT),
        name="out_proj",
    )(hm, ha, w_out, x2d, n2)


def _ffn_act_kernel(u_ref, wg_ref, wu_ref, a_ref, wg_scr, wu_scr):
    @pl.when(pl.program_id(1) == 0)
    def _():
        wg_scr[...] = wg_ref[...].astype(BF16)
        wu_scr[...] = wu_ref[...].astype(BF16)

    u = u_ref[...]
    g = jnp.dot(u, wg_scr[...], preferred_element_type=F32)
    up = jnp.dot(u, wu_scr[...], preferred_element_type=F32)
    a_ref[...] = (g * _sigmoid(g) * up).astype(BF16)


def _ffn_act(u2, wg, wu, tm=1024, tf=512):
    t = u2.shape[0]
    return pl.pallas_call(
        _ffn_act_kernel,
        grid=(D_FF // tf, t // tm),
        in_specs=[
            pl.BlockSpec((tm, D_MODEL), lambda f, i: (i, 0)),
            pl.BlockSpec((D_MODEL, tf), lambda f, i: (0, f)),
            pl.BlockSpec((D_MODEL, tf), lambda f, i: (0, f)),
        ],
        out_specs=pl.BlockSpec((tm, tf), lambda f, i: (i, f)),
        out_shape=jax.ShapeDtypeStruct((t, D_FF), BF16),
        scratch_shapes=[pltpu.VMEM((D_MODEL, tf), BF16), pltpu.VMEM((D_MODEL, tf), BF16)],
        compiler_params=pltpu.CompilerParams(
            dimension_semantics=("parallel", "arbitrary"), vmem_limit_bytes=VMEM_LIMIT),
        name="ffn_act",
    )(u2, wg, wu)


def _ffn_down_kernel(a_ref, wd_ref, h_ref, o_ref):
    o_ref[...] = h_ref[...] + jnp.dot(a_ref[...], wd_ref[...], preferred_element_type=F32)


def _ffn_down(act, wd, h1, tm=512, tn=1024):
    t = act.shape[0]
    return pl.pallas_call(
        _ffn_down_kernel,
        grid=(D_MODEL // tn, t // tm),
        in_specs=[
            pl.BlockSpec((tm, D_FF), lambda j, i: (i, 0)),
            pl.BlockSpec((D_FF, tn), lambda j, i: (0, j)),
            pl.BlockSpec((tm, tn), lambda j, i: (i, j)),
        ],
        out_specs=pl.BlockSpec((tm, tn), lambda j, i: (i, j)),
        out_shape=jax.ShapeDtypeStruct((t, D_MODEL), F32),
        compiler_params=pltpu.CompilerParams(
            dimension_semantics=("parallel", "parallel"), vmem_limit_bytes=VMEM_LIMIT),
        name="ffn_down",
    )(act, wd, h1)


def _rope_tables(s):
    f32 = np.float32
    inv_freq = f32(ROPE_THETA) ** (-np.arange(0, A_HEAD_DIM, 2, dtype=f32) / f32(A_HEAD_DIM))
    ang = np.arange(s, dtype=f32)[:, None] * inv_freq[None, :]
    cos, sin = np.cos(ang).astype(f32), np.sin(ang).astype(f32)

    def grouped(t):
        return np.ascontiguousarray(
            t.reshape(s // ATTN_RESIDUES, ATTN_RESIDUES, A_HEAD_DIM).transpose(1, 0, 2)).reshape(
                s, A_HEAD_DIM)

    return (jnp.asarray(grouped(np.concatenate([cos, cos], axis=1))),
            jnp.asarray(grouped(np.concatenate([-sin, sin], axis=1))))


def _pad_lanes(a, width):
    return jnp.pad(a, ((0, 0), (0, width - a.shape[1])))


def kernel(x, norm1_w, w_in, conv_w, conv_b, igate_b, fgate_b, q_norm_w, k_norm_w,
           mlstm_norm_w, attn_norm_w, w_out, norm2_w, w_gate, w_up, w_down):
    b, s, d = x.shape
    depth = norm1_w.shape[0]
    cos_full, sin_signed = _rope_tables(s)
    h = x
    for l in range(depth):
        w_main, w_gates = _w_in_prep(w_in[l])
        gate_b = jnp.concatenate(
            [_pad_lanes(igate_b[l][None, :], LANES), _pad_lanes(fgate_b[l][None, :], LANES)], axis=1)

        x2d = h.reshape(b * s, d)
        proj, gates = _in_proj(x2d, norm1_w[l][None, :], w_main, w_gates)
        proj3 = proj.reshape(b, s, PROJ_WIDTH)
        gates3 = gates.reshape(b, s, GATE_WIDTH)

        hm = _mlstm(proj3, gates3, conv_w[l], conv_b[l][None, :], gate_b,
                    mlstm_norm_w[l].reshape(1, M_V_WIDTH))
        ha = _attention(proj3, cos_full, sin_signed, q_norm_w[l][None, :], k_norm_w[l][None, :],
                        attn_norm_w[l].reshape(1, A_WIDTH))

        h1, u2 = _out_proj(hm.reshape(b * s, M_V_WIDTH), ha.reshape(b * s, A_WIDTH),
                           w_out[l].astype(BF16), x2d, norm2_w[l][None, :])
        act = _ffn_act(u2, w_gate[l], w_up[l])
        out = _ffn_down(act, w_down[l].astype(BF16), h1)
        h = out.reshape(b, s, d)
    return h
```

```python
import numpy as np
import jax
import jax.numpy as jnp
from jax import lax
from jax.experimental import pallas as pl
from jax.experimental.pallas import tpu as pltpu

D_MODEL = 2048
M_HEADS = 4
A_HEADS = 8
M_V_DIM = 256
M_QK_DIM = 128
A_HEAD_DIM = 128
CONV_WIDTH = 4
GATE_CAP = 15.0
ROPE_THETA = 10000.0
NORM_EPS = 1e-6
LOG2_E = 1.4426950408889634
D_FF = 5632
M_QK_WIDTH = M_HEADS * M_QK_DIM
M_V_WIDTH = M_HEADS * M_V_DIM
A_WIDTH = A_HEADS * A_HEAD_DIM
PROJ_WIDTH = 2 * M_QK_WIDTH + 2 * M_V_WIDTH + 3 * A_WIDTH
GATE_WIDTH = 256

LANES = 128
SUBLANES = 8
VMEM_LIMIT = 56 * 1024 * 1024

MLSTM_CHUNK = 256
ATTN_WINDOW_SUB = 128
ATTN_DILATIONS = (1, 4, 16)
ATTN_RESIDUES = 16
ATTN_GROUP = 8

BF16 = jnp.bfloat16
F32 = jnp.float32


def _sigmoid(x):
    return 1.0 / (1.0 + jnp.exp(-x))


def _rms(x, w):
    return x * lax.rsqrt(jnp.mean(x * x, axis=-1, keepdims=True) + NORM_EPS) * w


def _w_in_prep_kernel(left_ref, tail_ref, g_ref, main_ref, gate_ref):
    main_ref[0] = left_ref[...].astype(BF16)
    main_ref[1] = tail_ref[...].astype(BF16)

    @pl.when(pl.program_id(0) == 0)
    def _():
        g = g_ref[...]
        row = lax.broadcasted_iota(jnp.int32, g.shape, 0)
        zero = jnp.zeros_like(g)
        gate_ref[...] = jnp.zeros_like(gate_ref)
        gate_ref[0:2 * SUBLANES, :] = jnp.concatenate(
            [jnp.where(row < M_HEADS, g, 0.0), zero], axis=0).astype(BF16)
        gate_ref[LANES:LANES + 2 * SUBLANES, :] = jnp.concatenate(
            [jnp.where(row < M_HEADS, pltpu.roll(g, M_HEADS, 0), 0.0), zero], axis=0).astype(BF16)


def _w_in_prep(w_t, tr=256):
    n, k = w_t.shape
    g0 = 2 * M_QK_WIDTH + 2 * M_V_WIDTH
    half = PROJ_WIDTH // 2
    assert g0 == half and n == PROJ_WIDTH + 2 * M_HEADS
    return pl.pallas_call(
        _w_in_prep_kernel,
        grid=(half // tr,),
        in_specs=[
            pl.BlockSpec((tr, k), lambda i: (i, 0)),
            pl.BlockSpec((pl.Element(tr), pl.Element(k)),
                         lambda i: (pl.multiple_of(g0 + 2 * M_HEADS + i * tr, SUBLANES), 0)),
            pl.BlockSpec((pl.Element(2 * M_HEADS), pl.Element(k)), lambda i: (g0, 0)),
        ],
        out_specs=[
            pl.BlockSpec((2, tr, k), lambda i: (0, i, 0)),
            pl.BlockSpec((GATE_WIDTH, k), lambda i: (0, 0)),
        ],
        out_shape=[
            jax.ShapeDtypeStruct((2, half, k), BF16),
            jax.ShapeDtypeStruct((GATE_WIDTH, k), BF16),
        ],
        compiler_params=pltpu.CompilerParams(
            dimension_semantics=("arbitrary",), vmem_limit_bytes=VMEM_LIMIT),
        name="w_in_prep",
    )(w_t, w_t, w_t)


_NT = (((1,), (1,)), ((), ()))


def _in_proj_kernel(x_ref, n1_ref, w_ref, wg_ref, o_ref, og_ref, u_scr):
    @pl.when(pl.program_id(1) == 0)
    def _():
        u = _rms(x_ref[...], n1_ref[...]).astype(BF16)
        u_scr[...] = u
        og_ref[...] = lax.dot_general(u, wg_ref[...], _NT, preferred_element_type=F32)

    o_ref[...] = lax.dot_general(u_scr[...], w_ref[...], _NT, preferred_element_type=F32)


def _in_proj(x2d, n1, w_main, w_gate, tm=1024, tn=1024):
    t = x2d.shape[0]
    return pl.pallas_call(
        _in_proj_kernel,
        grid=(t // tm, PROJ_WIDTH // tn),
        in_specs=[
            pl.BlockSpec((tm, D_MODEL), lambda i, j: (i, 0)),
            pl.BlockSpec((1, D_MODEL), lambda i, j: (0, 0)),
            pl.BlockSpec((tn, D_MODEL), lambda i, j: (j, 0)),
            pl.BlockSpec((GATE_WIDTH, D_MODEL), lambda i, j: (0, 0)),
        ],
        out_specs=[
            pl.BlockSpec((tm, tn), lambda i, j: (i, j)),
            pl.BlockSpec((tm, GATE_WIDTH), lambda i, j: (i, 0)),
        ],
        out_shape=[
            jax.ShapeDtypeStruct((t, PROJ_WIDTH), F32),
            jax.ShapeDtypeStruct((t, GATE_WIDTH), F32),
        ],
        scratch_shapes=[pltpu.VMEM((tm, D_MODEL), BF16)],
        compiler_params=pltpu.CompilerParams(
            dimension_semantics=("parallel", "arbitrary"), vmem_limit_bytes=VMEM_LIMIT),
        name="in_proj",
    )(x2d, n1, w_main, w_gate)


def _mlstm_kernel(qk_ref, qkprev_ref, v_ref, o_ref, g_ref, cw_ref, cb_ref, gb_ref, nw_ref,
                  out_ref, xbuf, c_scr, m_scr):
    L = MLSTM_CHUNK
    c = pl.program_id(1)

    @pl.when(c == 0)
    def _():
        c_scr[...] = jnp.zeros_like(c_scr)
        m_scr[...] = jnp.zeros_like(m_scr)

    xbuf[0:SUBLANES, :] = jnp.where(c > 0, qkprev_ref[0], 0.0)
    xbuf[SUBLANES:, :] = qk_ref[0]
    y = cb_ref[...]
    for j in range(CONV_WIDTH):
        y = y + cw_ref[j:j + 1, :] * xbuf[pl.ds(SUBLANES - (CONV_WIDTH - 1) + j, L), :]
    qk = y * _sigmoid(y)

    gt = g_ref[0]
    i_pre = GATE_CAP * jnp.tanh((gt[:, :LANES] + gb_ref[:, :LANES]) / GATE_CAP)
    f_pre = GATE_CAP * jnp.tanh((gt[:, LANES:] + gb_ref[:, LANES:]) / GATE_CAP)
    log_f = jnp.minimum(f_pre, 0.0) - jnp.log1p(jnp.exp(-jnp.abs(f_pre)))

    row = lax.broadcasted_iota(jnp.int32, (L, L), 0)
    col = lax.broadcasted_iota(jnp.int32, (L, L), 1)
    causal = col <= row
    tril = jnp.where(causal, 1.0, 0.0).astype(BF16)
    hi = log_f.astype(BF16)
    r1 = log_f - hi.astype(F32)
    mid = r1.astype(BF16)
    lo = (r1 - mid.astype(F32)).astype(BF16)
    cum = (jnp.dot(tril, hi, preferred_element_type=F32)
           + jnp.dot(tril, mid, preferred_element_type=F32)
           + jnp.dot(tril, lo, preferred_element_type=F32))

    g_tot = cum[L - 1:L, :]
    a = g_tot - cum + i_pre
    m_loc = jnp.max(a, axis=0, keepdims=True)
    w_state = jnp.exp(a - m_loc)
    m_prev = m_scr[...]
    m_new = jnp.maximum(g_tot + m_prev, m_loc)
    s_old = jnp.exp(g_tot + m_prev - m_new)
    s_new = jnp.exp(m_loc - m_new)
    inter_log = cum + m_prev
    b_t = (i_pre - cum).T

    lane = lax.broadcasted_iota(jnp.int32, (L, LANES), 1)
    one_col = jnp.where(lane == 0, 1.0, 0.0).astype(BF16)
    vall = v_ref[0]
    oall = o_ref[0]
    for h in range(M_HEADS):
        q = (qk[:, h * M_QK_DIM:(h + 1) * M_QK_DIM] * (M_QK_DIM ** -0.5)).astype(BF16)
        kf = qk[:, M_QK_WIDTH + h * M_QK_DIM:M_QK_WIDTH + (h + 1) * M_QK_DIM]
        k = kf.astype(BF16)
        v_aug = jnp.concatenate(
            [vall[:, h * M_V_DIM:(h + 1) * M_V_DIM].astype(BF16), one_col], axis=1)

        d_log = jnp.where(causal, cum[:, h:h + 1] + b_t[h:h + 1, :], -jnp.inf)
        il = inter_log[:, h:h + 1]
        m_t = jnp.maximum(il, jnp.max(d_log, axis=1, keepdims=True))
        p = jnp.exp(d_log - m_t)
        w_inter = jnp.exp(il - m_t)
        s = lax.dot_general(q, k, (((1,), (1,)), ((), ())), preferred_element_type=F32)
        s_qk = (s * p).astype(BF16)

        c_prev = c_scr[h]
        tot = (w_inter * jnp.dot(q, c_prev.astype(BF16), preferred_element_type=F32)
               + jnp.dot(s_qk, v_aug, preferred_element_type=F32))
        num = tot[:, :M_V_DIM]
        den = tot[:, M_V_DIM:M_V_DIM + 1]
        hout = num / jnp.maximum(jnp.abs(den), jnp.exp(-m_t))
        hn = _rms(hout, nw_ref[:, h * M_V_DIM:(h + 1) * M_V_DIM])
        gate = _sigmoid(oall[:, h * M_V_DIM:(h + 1) * M_V_DIM])
        out_ref[0, :, h * M_V_DIM:(h + 1) * M_V_DIM] = (hn * gate).astype(BF16)

        kw_t = (kf * w_state[:, h:h + 1]).T.astype(BF16)
        kv = jnp.dot(kw_t, v_aug, preferred_element_type=F32)
        c_scr[h] = s_old[:, h:h + 1] * c_prev + s_new[:, h:h + 1] * kv
    m_scr[...] = m_new


def _mlstm(proj3, gates3, conv_w, conv_b, gate_b, norm_w):
    b, s, _ = proj3.shape
    L = MLSTM_CHUNK
    qk_w = 2 * M_QK_WIDTH
    return pl.pallas_call(
        _mlstm_kernel,
        grid=(b, s // L),
        in_specs=[
            pl.BlockSpec((1, L, qk_w), lambda i, c: (i, c, 0)),
            pl.BlockSpec((1, SUBLANES, qk_w),
                         lambda i, c: (i, jnp.maximum(c * (L // SUBLANES) - 1, 0), 0)),
            pl.BlockSpec((1, L, M_V_WIDTH), lambda i, c: (i, c, qk_w // M_V_WIDTH)),
            pl.BlockSpec((1, L, M_V_WIDTH), lambda i, c: (i, c, qk_w // M_V_WIDTH + 1)),
            pl.BlockSpec((1, L, GATE_WIDTH), lambda i, c: (i, c, 0)),
            pl.BlockSpec((CONV_WIDTH, qk_w), lambda i, c: (0, 0)),
            pl.BlockSpec((1, qk_w), lambda i, c: (0, 0)),
            pl.BlockSpec((1, GATE_WIDTH), lambda i, c: (0, 0)),
            pl.BlockSpec((1, M_V_WIDTH), lambda i, c: (0, 0)),
        ],
        out_specs=pl.BlockSpec((1, L, M_V_WIDTH), lambda i, c: (i, c, 0)),
        out_shape=jax.ShapeDtypeStruct((b, s, M_V_WIDTH), BF16),
        scratch_shapes=[
            pltpu.VMEM((L + SUBLANES, qk_w), F32),
            pltpu.VMEM((M_HEADS, M_QK_DIM, M_V_DIM + LANES), F32),
            pltpu.VMEM((1, LANES), F32),
        ],
        compiler_params=pltpu.CompilerParams(
            dimension_semantics=("parallel", "arbitrary"), vmem_limit_bytes=VMEM_LIMIT),
        name="mlstm",
    )(proj3, proj3, proj3, proj3, gates3, conv_w, conv_b, gate_b, norm_w)


def _attn_kernel(q_ref, k_ref, v_ref, cos_ref, sin_ref, qw_ref, kw_ref, aw_ref, out_ref,
                 qs, ks, vs, acc, ms, ls):
    s_len = q_ref.shape[1]
    R = ATTN_RESIDUES
    per = s_len // R
    W = ATTN_WINDOW_SUB

    ones = jnp.ones((A_HEAD_DIM, LANES), BF16)

    def prep(r, carry):
        rows = pl.ds(r, per, stride=R)
        dst = pl.ds(pl.multiple_of(r * per, per), per)
        cos = cos_ref[dst, :]
        sin = sin_ref[dst, :]

        def rope(x, w):
            sq = x * x
            hi = sq.astype(BF16)
            lo = (sq - hi.astype(F32)).astype(BF16)
            ssq = (jnp.dot(hi, ones, preferred_element_type=F32)
                   + jnp.dot(lo, ones, preferred_element_type=F32))
            xn = x * lax.rsqrt(ssq * (1.0 / A_HEAD_DIM) + NORM_EPS) * w
            return xn * cos + pltpu.roll(xn, A_HEAD_DIM // 2, 1) * sin

        qs[dst, :] = rope(q_ref[0, rows, :], qw_ref[...]) * (A_HEAD_DIM ** -0.5 * LOG2_E)
        ks[dst, :] = rope(k_ref[0, rows, :], kw_ref[...])
        vs[dst, :] = v_ref[0, rows, :]
        return carry

    lax.fori_loop(0, R, prep, 0, unroll=4)

    pos_q = lax.broadcasted_iota(jnp.int32, (W, W), 0)
    pos_k = lax.broadcasted_iota(jnp.int32, (W, W), 1)
    nt = (((1,), (1,)), ((), ()))
    G = ATTN_GROUP

    for pi, d in enumerate(ATTN_DILATIONS):
        run = SUBLANES * d
        nruns = W // run
        shift = run.bit_length() - 1

        def sub_pos(pos, run=run, nruns=nruns, shift=shift):
            return (pos & (run - 1)) * nruns + (pos >> shift)

        uq = sub_pos(pos_q)
        uk = sub_pos(pos_k)
        bias_cur = jnp.where(uk <= uq, 0.0, -jnp.inf).astype(F32)
        bias_prev = jnp.where(uk >= uq, 0.0, -jnp.inf).astype(F32)

        def group(gi, carry, d=d, run=run, nruns=nruns, bias_cur=bias_cur, bias_prev=bias_prev,
                  first=(pi == 0)):
            if d == 16:
                slab = [pl.multiple_of(gi * (G * W), G * W)]
            elif d == 4:
                slab = [pl.multiple_of(gi * per + a * (d * per), per) for a in range(nruns)]
            else:
                slab = [pl.multiple_of(a * per + gi * (G * run), G * run) for a in range(nruns)]

            def blocks(ref):
                slabs = [ref[pl.ds(st, G * run), :] for st in slab]
                return [jnp.concatenate([sl[g * run:(g + 1) * run, :] for sl in slabs], axis=0)
                        for g in range(G)]

            qg = [x.astype(BF16) for x in blocks(qs)]
            kg = [x.astype(BF16) for x in blocks(ks)]
            vg = [x.astype(BF16) for x in blocks(vs)]
            if d == 1:
                pst = [pl.multiple_of(a * per + jnp.maximum(gi * (G * run) - run, 0), run)
                       for a in range(nruns)]
                k_before = jnp.concatenate([ks[pl.ds(st, run), :] for st in pst], axis=0).astype(BF16)
                v_before = jnp.concatenate([vs[pl.ds(st, run), :] for st in pst], axis=0).astype(BF16)
                before_bias = jnp.where(gi > 0, 0.0, -jnp.inf)
            if not first:
                m_old_g, l_old_g, a_old_g = blocks(ms), blocks(ls), blocks(acc)

            if d == 16:
                has_prev = [g % 2 == 1 for g in range(G)]
            elif d == 4:
                has_prev = [g > 0 for g in range(G)]
            else:
                has_prev = [True] * G
            k_prev = [(k_before if d == 1 else None) if g == 0 else kg[g - 1] for g in range(G)]
            v_prev = [(v_before if d == 1 else None) if g == 0 else vg[g - 1] for g in range(G)]

            s_cur = [lax.dot_general(qg[g], kg[g], nt, preferred_element_type=F32) for g in range(G)]
            s_prv = [lax.dot_general(qg[g], k_prev[g], nt, preferred_element_type=F32)
                     if has_prev[g] else None for g in range(G)]
            p_cur, p_prv, m_blk, l_blk = [], [], [], []
            for g in range(G):
                s_c = s_cur[g] + bias_cur
                if has_prev[g]:
                    s_p = s_prv[g] + bias_prev
                    if g == 0:
                        s_p = s_p + before_bias
                    m = jnp.max(jnp.maximum(s_c, s_p), axis=1, keepdims=True)
                    p_c = jnp.exp2(s_c - m)
                    p_p = jnp.exp2(s_p - m)
                    l = jnp.sum(p_c + p_p, axis=1, keepdims=True)
                    p_prv.append(p_p.astype(BF16))
                else:
                    m = jnp.max(s_c, axis=1, keepdims=True)
                    p_c = jnp.exp2(s_c - m)
                    l = jnp.sum(p_c, axis=1, keepdims=True)
                    p_prv.append(None)
                p_cur.append(p_c.astype(BF16))
                m_blk.append(m)
                l_blk.append(l)
            o_blk = []
            for g in range(G):
                o = jnp.dot(p_cur[g], vg[g], preferred_element_type=F32)
                if has_prev[g]:
                    o = o + jnp.dot(p_prv[g], v_prev[g], preferred_element_type=F32)
                o_blk.append(o)

            a_out, m_out, l_out = [], [], []
            for g in range(G):
                if first:
                    a_out.append(o_blk[g])
                    m_out.append(jnp.broadcast_to(m_blk[g], (W, LANES)))
                    l_out.append(jnp.broadcast_to(l_blk[g], (W, LANES)))
                else:
                    m_new = jnp.maximum(m_old_g[g], m_blk[g])
                    e_old = jnp.exp2(m_old_g[g] - m_new)
                    e_blk = jnp.exp2(m_blk[g] - m_new)
                    a_out.append(a_old_g[g] * e_old + o_blk[g] * e_blk)
                    l_out.append(l_old_g[g] * e_old + l_blk[g] * e_blk)
                    m_out.append(m_new)

            for ref, vals in ((acc, a_out), (ms, m_out), (ls, l_out)):
                for a, st in enumerate(slab):
                    ref[pl.ds(st, G * run), :] = jnp.concatenate(
                        [v[a * run:(a + 1) * run, :] for v in vals], axis=0)
            return carry

        lax.fori_loop(0, s_len // (G * W), group, 0)

    def finish(r, carry):
        src = pl.ds(pl.multiple_of(r * per, per), per)
        o = acc[src, :] / ls[src, :]
        qs[pl.ds(r, per, stride=R), :] = _rms(o, aw_ref[...])
        return carry

    lax.fori_loop(0, R, finish, 0, unroll=4)
    out_ref[0] = qs[...].astype(BF16)


def _attention(proj3, cos_full, sin_signed, q_norm_w, k_norm_w, attn_norm_w):
    b, s, _ = proj3.shape
    base = (2 * M_QK_WIDTH + 2 * M_V_WIDTH) // A_HEAD_DIM
    nh = A_HEADS
    row_spec = lambda off: pl.BlockSpec((1, s, A_HEAD_DIM), lambda i, h: (i, 0, base + off + h))
    return pl.pallas_call(
        _attn_kernel,
        grid=(b, nh),
        in_specs=[
            row_spec(0), row_spec(nh), row_spec(2 * nh),
            pl.BlockSpec((s, A_HEAD_DIM), lambda i, h: (0, 0)),
            pl.BlockSpec((s, A_HEAD_DIM), lambda i, h: (0, 0)),
            pl.BlockSpec((1, A_HEAD_DIM), lambda i, h: (0, 0)),
            pl.BlockSpec((1, A_HEAD_DIM), lambda i, h: (0, 0)),
            pl.BlockSpec((1, A_HEAD_DIM), lambda i, h: (0, h)),
        ],
        out_specs=pl.BlockSpec((1, s, A_HEAD_DIM), lambda i, h: (i, 0, h)),
        out_shape=jax.ShapeDtypeStruct((b, s, A_WIDTH), BF16),
        scratch_shapes=[pltpu.VMEM((s, A_HEAD_DIM), F32) for _ in range(6)],
        compiler_params=pltpu.CompilerParams(
            dimension_semantics=("parallel", "parallel"), vmem_limit_bytes=VMEM_LIMIT),
        name="attention",
    )(proj3, proj3, proj3, cos_full, sin_signed, q_norm_w, k_norm_w, attn_norm_w)


def _out_proj_kernel(hm_ref, ha_ref, w_ref, x_ref, n2_ref, h_ref, u_ref):
    h1 = (x_ref[...]
          + jnp.dot(hm_ref[...], w_ref[:M_V_WIDTH, :], preferred_element_type=F32)
          + jnp.dot(ha_ref[...], w_ref[M_V_WIDTH:, :], preferred_element_type=F32))
    h_ref[...] = h1
    u_ref[...] = _rms(h1, n2_ref[...]).astype(BF16)


def _out_proj(hm, ha, w_out, x2d, n2, tm=512):
    t = x2d.shape[0]
    return pl.pallas_call(
        _out_proj_kernel,
        grid=(t // tm,),
        in_specs=[
            pl.BlockSpec((tm, M_V_WIDTH), lambda i: (i, 0)),
            pl.BlockSpec((tm, A_WIDTH), lambda i: (i, 0)),
            pl.BlockSpec((D_MODEL, D_MODEL), lambda i: (0, 0)),
            pl.BlockSpec((tm, D_MODEL), lambda i: (i, 0)),
            pl.BlockSpec((1, D_MODEL), lambda i: (0, 0)),
        ],
        out_specs=[
            pl.BlockSpec((tm, D_MODEL), lambda i: (i, 0)),
            pl.BlockSpec((tm, D_MODEL), lambda i: (i, 0)),
        ],
        out_shape=[
            jax.ShapeDtypeStruct((t, D_MODEL), F32),
            jax.ShapeDtypeStruct((t, D_MODEL), BF16),
        ],
        compiler_params=pltpu.CompilerParams(
            dimension_semantics=("parallel",), vmem_limit_bytes=VMEM_LIMIT),
        name="out_proj",
    )(hm, ha, w_out, x2d, n2)


def _ffn_act_kernel(u_ref, wg_ref, wu_ref, a_ref, wg_scr, wu_scr):
    @pl.when(pl.program_id(1) == 0)
    def _():
        wg_scr[...] = wg_ref[...].astype(BF16)
        wu_scr[...] = wu_ref[...].astype(BF16)

    u = u_ref[...]
    g = jnp.dot(u, wg_scr[...], preferred_element_type=F32)
    up = jnp.dot(u, wu_scr[...], preferred_element_type=F32)
    a_ref[...] = (g * _sigmoid(g) * up).astype(BF16)


def _ffn_act(u2, wg, wu, tm=1024, tf=512):
    t = u2.shape[0]
    return pl.pallas_call(
        _ffn_act_kernel,
        grid=(D_FF // tf, t // tm),
        in_specs=[
            pl.BlockSpec((tm, D_MODEL), lambda f, i: (i, 0)),
            pl.BlockSpec((D_MODEL, tf), lambda f, i: (0, f)),
            pl.BlockSpec((D_MODEL, tf), lambda f, i: (0, f)),
        ],
        out_specs=pl.BlockSpec((tm, tf), lambda f, i: (i, f)),
        out_shape=jax.ShapeDtypeStruct((t, D_FF), BF16),
        scratch_shapes=[pltpu.VMEM((D_MODEL, tf), BF16), pltpu.VMEM((D_MODEL, tf), BF16)],
        compiler_params=pltpu.CompilerParams(
            dimension_semantics=("parallel", "arbitrary"), vmem_limit_bytes=VMEM_LIMIT),
        name="ffn_act",
    )(u2, wg, wu)


def _ffn_down_kernel(a_ref, wd_ref, h_ref, o_ref):
    o_ref[...] = h_ref[...] + jnp.dot(a_ref[...], wd_ref[...], preferred_element_type=F32)


def _ffn_down(act, wd, h1, tm=512, tn=1024):
    t = act.shape[0]
    return pl.pallas_call(
        _ffn_down_kernel,
        grid=(D_MODEL // tn, t // tm),
        in_specs=[
            pl.BlockSpec((tm, D_FF), lambda j, i: (i, 0)),
            pl.BlockSpec((D_FF, tn), lambda j, i: (0, j)),
            pl.BlockSpec((tm, tn), lambda j, i: (i, j)),
        ],
        out_specs=pl.BlockSpec((tm, tn), lambda j, i: (i, j)),
        out_shape=jax.ShapeDtypeStruct((t, D_MODEL), F32),
        compiler_params=pltpu.CompilerParams(
            dimension_semantics=("parallel", "parallel"), vmem_limit_bytes=VMEM_LIMIT),
        name="ffn_down",
    )(act, wd, h1)


def _rope_tables(s):
    f32 = np.float32
    inv_freq = f32(ROPE_THETA) ** (-np.arange(0, A_HEAD_DIM, 2, dtype=f32) / f32(A_HEAD_DIM))
    ang = np.arange(s, dtype=f32)[:, None] * inv_freq[None, :]
    cos, sin = np.cos(ang).astype(f32), np.sin(ang).astype(f32)

    def grouped(t):
        return np.ascontiguousarray(
            t.reshape(s // ATTN_RESIDUES, ATTN_RESIDUES, A_HEAD_DIM).transpose(1, 0, 2)).reshape(
                s, A_HEAD_DIM)

    return (jnp.asarray(grouped(np.concatenate([cos, cos], axis=1))),
            jnp.asarray(grouped(np.concatenate([-sin, sin], axis=1))))


def _pad_lanes(a, width):
    return jnp.pad(a, ((0, 0), (0, width - a.shape[1])))


def kernel(x, norm1_w, w_in, conv_w, conv_b, igate_b, fgate_b, q_norm_w, k_norm_w,
           mlstm_norm_w, attn_norm_w, w_out, norm2_w, w_gate, w_up, w_down):
    b, s, d = x.shape
    depth = norm1_w.shape[0]
    cos_full, sin_signed = _rope_tables(s)
    h = x
    for l in range(depth):
        w_main, w_gates = _w_in_prep(w_in[l].T)
        w_main = w_main.reshape(PROJ_WIDTH, d)
        gate_b = jnp.concatenate(
            [_pad_lanes(igate_b[l][None, :], LANES), _pad_lanes(fgate_b[l][None, :], LANES)], axis=1)

        x2d = h.reshape(b * s, d)
        proj, gates = _in_proj(x2d, norm1_w[l][None, :], w_main, w_gates)
        proj3 = proj.reshape(b, s, PROJ_WIDTH)
        gates3 = gates.reshape(b, s, GATE_WIDTH)

        hm = _mlstm(proj3, gates3, conv_w[l], conv_b[l][None, :], gate_b,
                    mlstm_norm_w[l].reshape(1, M_V_WIDTH))
        ha = _attention(proj3, cos_full, sin_signed, q_norm_w[l][None, :], k_norm_w[l][None, :],
                        attn_norm_w[l].reshape(1, A_WIDTH))

        h1, u2 = _out_proj(hm.reshape(b * s, M_V_WIDTH), ha.reshape(b * s, A_WIDTH),
                           w_out[l].astype(BF16), x2d, norm2_w[l][None, :])
        act = _ffn_act(u2, w_gate[l], w_up[l])
        out = _ffn_down(act, w_down[l].astype(BF16), h1)
        h = out.reshape(b, s, d)
    return h
```

```python
import numpy as np
import jax
import jax.numpy as jnp
from jax import lax
from jax.experimental import pallas as pl
from jax.experimental.pallas import tpu as pltpu

D_MODEL = 2048
M_HEADS = 4
A_HEADS = 8
M_V_DIM = 256
M_QK_DIM = 128
A_HEAD_DIM = 128
CONV_WIDTH = 4
GATE_CAP = 15.0
ROPE_THETA = 10000.0
NORM_EPS = 1e-6
LOG2_E = 1.4426950408889634
D_FF = 5632
M_QK_WIDTH = M_HEADS * M_QK_DIM
M_V_WIDTH = M_HEADS * M_V_DIM
A_WIDTH = A_HEADS * A_HEAD_DIM
PROJ_WIDTH = 2 * M_QK_WIDTH + 2 * M_V_WIDTH + 3 * A_WIDTH
GATE_WIDTH = 256

LANES = 128
SUBLANES = 8
VMEM_LIMIT = 56 * 1024 * 1024

MLSTM_CHUNK = 256
ATTN_WINDOW_SUB = 128
ATTN_DILATIONS = (1, 4, 16)
ATTN_RESIDUES = 16
ATTN_GROUP = 8

BF16 = jnp.bfloat16
F32 = jnp.float32


def _sigmoid(x):
    return 1.0 / (1.0 + jnp.exp(-x))


def _rms(x, w):
    return x * lax.rsqrt(jnp.mean(x * x, axis=-1, keepdims=True) + NORM_EPS) * w


def _w_in_prep_kernel(left_ref, tail_ref, g_ref, main_ref, gate_ref):
    main_ref[0] = left_ref[...].astype(BF16)
    main_ref[1] = tail_ref[...].astype(BF16)

    @pl.when(pl.program_id(0) == 0)
    def _():
        g = g_ref[...]
        row = lax.broadcasted_iota(jnp.int32, g.shape, 0)
        zero = jnp.zeros_like(g)
        gate_ref[...] = jnp.zeros_like(gate_ref)
        gate_ref[0:2 * SUBLANES, :] = jnp.concatenate(
            [jnp.where(row < M_HEADS, g, 0.0), zero], axis=0).astype(BF16)
        gate_ref[LANES:LANES + 2 * SUBLANES, :] = jnp.concatenate(
            [jnp.where(row < M_HEADS, pltpu.roll(g, M_HEADS, 0), 0.0), zero], axis=0).astype(BF16)


def _w_in_prep(w_t, tr=256):
    n, k = w_t.shape
    g0 = 2 * M_QK_WIDTH + 2 * M_V_WIDTH
    half = PROJ_WIDTH // 2
    assert g0 == half and n == PROJ_WIDTH + 2 * M_HEADS
    return pl.pallas_call(
        _w_in_prep_kernel,
        grid=(half // tr,),
        in_specs=[
            pl.BlockSpec((tr, k), lambda i: (i, 0)),
            pl.BlockSpec((pl.Element(tr), pl.Element(k)),
                         lambda i: (pl.multiple_of(g0 + 2 * M_HEADS + i * tr, SUBLANES), 0)),
            pl.BlockSpec((pl.Element(2 * M_HEADS), pl.Element(k)), lambda i: (g0, 0)),
        ],
        out_specs=[
            pl.BlockSpec((2, tr, k), lambda i: (0, i, 0)),
            pl.BlockSpec((GATE_WIDTH, k), lambda i: (0, 0)),
        ],
        out_shape=[
            jax.ShapeDtypeStruct((2, half, k), BF16),
            jax.ShapeDtypeStruct((GATE_WIDTH, k), BF16),
        ],
        compiler_params=pltpu.CompilerParams(
            dimension_semantics=("arbitrary",), vmem_limit_bytes=VMEM_LIMIT),
        name="w_in_prep",
    )(w_t, w_t, w_t)


_NT = (((1,), (1,)), ((), ()))


def _in_proj_kernel(x_ref, n1_ref, w_ref, wg_ref, o_ref, og_ref, u_scr):
    @pl.when(pl.program_id(1) == 0)
    def _():
        u = _rms(x_ref[...], n1_ref[...]).astype(BF16)
        u_scr[...] = u
        og_ref[...] = lax.dot_general(u, wg_ref[...], _NT, preferred_element_type=F32)

    o_ref[...] = lax.dot_general(u_scr[...], w_ref[...], _NT, preferred_element_type=F32)


def _in_proj(x2d, n1, w_main, w_gate, tm=1024, tn=1024):
    t = x2d.shape[0]
    return pl.pallas_call(
        _in_proj_kernel,
        grid=(t // tm, PROJ_WIDTH // tn),
        in_specs=[
            pl.BlockSpec((tm, D_MODEL), lambda i, j: (i, 0)),
            pl.BlockSpec((1, D_MODEL), lambda i, j: (0, 0)),
            pl.BlockSpec((tn, D_MODEL), lambda i, j: (j, 0)),
            pl.BlockSpec((GATE_WIDTH, D_MODEL), lambda i, j: (0, 0)),
        ],
        out_specs=[
            pl.BlockSpec((tm, tn), lambda i, j: (i, j)),
            pl.BlockSpec((tm, GATE_WIDTH), lambda i, j: (i, 0)),
        ],
        out_shape=[
            jax.ShapeDtypeStruct((t, PROJ_WIDTH), F32),
            jax.ShapeDtypeStruct((t, GATE_WIDTH), F32),
        ],
        scratch_shapes=[pltpu.VMEM((tm, D_MODEL), BF16)],
        compiler_params=pltpu.CompilerParams(
            dimension_semantics=("parallel", "arbitrary"), vmem_limit_bytes=VMEM_LIMIT),
        name="in_proj",
    )(x2d, n1, w_main, w_gate)


def _mlstm_kernel(qk_ref, qkprev_ref, v_ref, o_ref, g_ref, cw_ref, cb_ref, gb_ref, nw_ref,
                  out_ref, xbuf, c_scr, m_scr):
    L = MLSTM_CHUNK
    c = pl.program_id(1)

    @pl.when(c == 0)
    def _():
        c_scr[...] = jnp.zeros_like(c_scr)
        m_scr[...] = jnp.zeros_like(m_scr)

    xbuf[0:SUBLANES, :] = jnp.where(c > 0, qkprev_ref[0], 0.0)
    xbuf[SUBLANES:, :] = qk_ref[0]
    y = cb_ref[...]
    for j in range(CONV_WIDTH):
        y = y + cw_ref[j:j + 1, :] * xbuf[pl.ds(SUBLANES - (CONV_WIDTH - 1) + j, L), :]
    qk = y * _sigmoid(y)

    gt = g_ref[0]
    i_pre = GATE_CAP * jnp.tanh((gt[:, :LANES] + gb_ref[:, :LANES]) / GATE_CAP)
    f_pre = GATE_CAP * jnp.tanh((gt[:, LANES:] + gb_ref[:, LANES:]) / GATE_CAP)
    log_f = jnp.minimum(f_pre, 0.0) - jnp.log1p(jnp.exp(-jnp.abs(f_pre)))

    row = lax.broadcasted_iota(jnp.int32, (L, L), 0)
    col = lax.broadcasted_iota(jnp.int32, (L, L), 1)
    causal = col <= row
    tril = jnp.where(causal, 1.0, 0.0).astype(BF16)
    hi = log_f.astype(BF16)
    r1 = log_f - hi.astype(F32)
    mid = r1.astype(BF16)
    lo = (r1 - mid.astype(F32)).astype(BF16)
    cum = (jnp.dot(tril, hi, preferred_element_type=F32)
           + jnp.dot(tril, mid, preferred_element_type=F32)
           + jnp.dot(tril, lo, preferred_element_type=F32))

    g_tot = cum[L - 1:L, :]
    a = g_tot - cum + i_pre
    m_loc = jnp.max(a, axis=0, keepdims=True)
    w_state = jnp.exp(a - m_loc)
    m_prev = m_scr[...]
    m_new = jnp.maximum(g_tot + m_prev, m_loc)
    s_old = jnp.exp(g_tot + m_prev - m_new)
    s_new = jnp.exp(m_loc - m_new)
    inter_log = cum + m_prev
    b_t = (i_pre - cum).T

    lane = lax.broadcasted_iota(jnp.int32, (L, LANES), 1)
    one_col = jnp.where(lane == 0, 1.0, 0.0).astype(BF16)
    vall = v_ref[0]
    oall = o_ref[0]
    heads = range(M_HEADS)
    vsl = [slice(h * M_V_DIM, (h + 1) * M_V_DIM) for h in heads]

    q = [(qk[:, h * M_QK_DIM:(h + 1) * M_QK_DIM] * (M_QK_DIM ** -0.5)).astype(BF16) for h in heads]
    kf = [qk[:, M_QK_WIDTH + h * M_QK_DIM:M_QK_WIDTH + (h + 1) * M_QK_DIM] for h in heads]
    v_aug = [jnp.concatenate([vall[:, vsl[h]].astype(BF16), one_col], axis=1) for h in heads]
    c_prev = [c_scr[h] for h in heads]
    s = [lax.dot_general(q[h], kf[h].astype(BF16), _NT, preferred_element_type=F32) for h in heads]
    inter = [jnp.dot(q[h], c_prev[h].astype(BF16), preferred_element_type=F32) for h in heads]
    kv = [jnp.dot((kf[h] * w_state[:, h:h + 1]).T.astype(BF16), v_aug[h],
                  preferred_element_type=F32) for h in heads]

    m_t, w_inter, s_qk = [], [], []
    for h in heads:
        d_log = jnp.where(causal, cum[:, h:h + 1] + b_t[h:h + 1, :], -jnp.inf)
        il = inter_log[:, h:h + 1]
        m = jnp.maximum(il, jnp.max(d_log, axis=1, keepdims=True))
        m_t.append(m)
        w_inter.append(jnp.exp(il - m))
        s_qk.append((s[h] * jnp.exp(d_log - m)).astype(BF16))

    intra = [jnp.dot(s_qk[h], v_aug[h], preferred_element_type=F32) for h in heads]

    for h in heads:
        tot = w_inter[h] * inter[h] + intra[h]
        num = tot[:, :M_V_DIM]
        den = tot[:, M_V_DIM:M_V_DIM + 1]
        hout = num / jnp.maximum(jnp.abs(den), jnp.exp(-m_t[h]))
        hn = _rms(hout, nw_ref[:, vsl[h]])
        out_ref[0, :, vsl[h]] = (hn * _sigmoid(oall[:, vsl[h]])).astype(BF16)
        c_scr[h] = s_old[:, h:h + 1] * c_prev[h] + s_new[:, h:h + 1] * kv[h]
    m_scr[...] = m_new


def _mlstm(proj3, gates3, conv_w, conv_b, gate_b, norm_w):
    b, s, _ = proj3.shape
    L = MLSTM_CHUNK
    qk_w = 2 * M_QK_WIDTH
    return pl.pallas_call(
        _mlstm_kernel,
        grid=(b, s // L),
        in_specs=[
            pl.BlockSpec((1, L, qk_w), lambda i, c: (i, c, 0)),
            pl.BlockSpec((1, SUBLANES, qk_w),
                         lambda i, c: (i, jnp.maximum(c * (L // SUBLANES) - 1, 0), 0)),
            pl.BlockSpec((1, L, M_V_WIDTH), lambda i, c: (i, c, qk_w // M_V_WIDTH)),
            pl.BlockSpec((1, L, M_V_WIDTH), lambda i, c: (i, c, qk_w // M_V_WIDTH + 1)),
            pl.BlockSpec((1, L, GATE_WIDTH), lambda i, c: (i, c, 0)),
            pl.BlockSpec((CONV_WIDTH, qk_w), lambda i, c: (0, 0)),
            pl.BlockSpec((1, qk_w), lambda i, c: (0, 0)),
            pl.BlockSpec((1, GATE_WIDTH), lambda i, c: (0, 0)),
            pl.BlockSpec((1, M_V_WIDTH), lambda i, c: (0, 0)),
        ],
        out_specs=pl.BlockSpec((1, L, M_V_WIDTH), lambda i, c: (i, c, 0)),
        out_shape=jax.ShapeDtypeStruct((b, s, M_V_WIDTH), BF16),
        scratch_shapes=[
            pltpu.VMEM((L + SUBLANES, qk_w), F32),
            pltpu.VMEM((M_HEADS, M_QK_DIM, M_V_DIM + LANES), F32),
            pltpu.VMEM((1, LANES), F32),
        ],
        compiler_params=pltpu.CompilerParams(
            dimension_semantics=("parallel", "arbitrary"), vmem_limit_bytes=VMEM_LIMIT),
        name="mlstm",
    )(proj3, proj3, proj3, proj3, gates3, conv_w, conv_b, gate_b, norm_w)


def _attn_kernel(*refs):
    R = ATTN_RESIDUES
    q_refs, k_refs, v_refs = refs[:R], refs[R:2 * R], refs[2 * R:3 * R]
    cos_ref, sin_ref, qw_ref, kw_ref, aw_ref, out_ref, qs, ks, vs, acc, ms, ls = refs[3 * R:]
    s_len = out_ref.shape[1]
    per = s_len // R
    W = ATTN_WINDOW_SUB

    ones = jnp.ones((A_HEAD_DIM, LANES), BF16)
    for r in range(R):
        dst = pl.ds(r * per, per)
        cos = cos_ref[dst, :]
        sin = sin_ref[dst, :]

        def rope(x, w):
            sq = x * x
            hi = sq.astype(BF16)
            lo = (sq - hi.astype(F32)).astype(BF16)
            ssq = (jnp.dot(hi, ones, preferred_element_type=F32)
                   + jnp.dot(lo, ones, preferred_element_type=F32))
            xn = x * lax.rsqrt(ssq * (1.0 / A_HEAD_DIM) + NORM_EPS) * w
            return xn * cos + pltpu.roll(xn, A_HEAD_DIM // 2, 1) * sin

        qs[dst, :] = rope(q_refs[r][0], qw_ref[...]) * (A_HEAD_DIM ** -0.5 * LOG2_E)
        ks[dst, :] = rope(k_refs[r][0], kw_ref[...])
        vs[dst, :] = v_refs[r][0]

    pos_q = lax.broadcasted_iota(jnp.int32, (W, W), 0)
    pos_k = lax.broadcasted_iota(jnp.int32, (W, W), 1)
    nt = (((1,), (1,)), ((), ()))
    G = ATTN_GROUP

    for pi, d in enumerate(ATTN_DILATIONS):
        run = SUBLANES * d
        nruns = W // run
        shift = run.bit_length() - 1

        def sub_pos(pos, run=run, nruns=nruns, shift=shift):
            return (pos & (run - 1)) * nruns + (pos >> shift)

        uq = sub_pos(pos_q)
        uk = sub_pos(pos_k)
        bias_cur = jnp.where(uk <= uq, 0.0, -jnp.inf).astype(F32)
        bias_prev = jnp.where(uk >= uq, 0.0, -jnp.inf).astype(F32)

        def group(gi, carry, d=d, run=run, nruns=nruns, bias_cur=bias_cur, bias_prev=bias_prev,
                  first=(pi == 0)):
            if d == 16:
                slab = [pl.multiple_of(gi * (G * W), G * W)]
            elif d == 4:
                slab = [pl.multiple_of(gi * per + a * (d * per), per) for a in range(nruns)]
            else:
                slab = [pl.multiple_of(a * per + gi * (G * run), G * run) for a in range(nruns)]

            def blocks(ref):
                slabs = [ref[pl.ds(st, G * run), :] for st in slab]
                return [jnp.concatenate([sl[g * run:(g + 1) * run, :] for sl in slabs], axis=0)
                        for g in range(G)]

            qg = [x.astype(BF16) for x in blocks(qs)]
            kg = [x.astype(BF16) for x in blocks(ks)]
            vg = [x.astype(BF16) for x in blocks(vs)]
            if d == 1:
                pst = [pl.multiple_of(a * per + jnp.maximum(gi * (G * run) - run, 0), run)
                       for a in range(nruns)]
                k_before = jnp.concatenate([ks[pl.ds(st, run), :] for st in pst], axis=0).astype(BF16)
                v_before = jnp.concatenate([vs[pl.ds(st, run), :] for st in pst], axis=0).astype(BF16)
                before_bias = jnp.where(gi > 0, 0.0, -jnp.inf)
            if not first:
                m_old_g, l_old_g, a_old_g = blocks(ms), blocks(ls), blocks(acc)

            if d == 16:
                has_prev = [g % 2 == 1 for g in range(G)]
            elif d == 4:
                has_prev = [g > 0 for g in range(G)]
            else:
                has_prev = [True] * G
            k_prev = [(k_before if d == 1 else None) if g == 0 else kg[g - 1] for g in range(G)]
            v_prev = [(v_before if d == 1 else None) if g == 0 else vg[g - 1] for g in range(G)]

            s_cur = [lax.dot_general(qg[g], kg[g], nt, preferred_element_type=F32) for g in range(G)]
            s_prv = [lax.dot_general(qg[g], k_prev[g], nt, preferred_element_type=F32)
                     if has_prev[g] else None for g in range(G)]
            p_cur, p_prv, m_blk, l_blk = [], [], [], []
            for g in range(G):
                s_c = s_cur[g] + bias_cur
                if has_prev[g]:
                    s_p = s_prv[g] + bias_prev
                    if g == 0:
                        s_p = s_p + before_bias
                    m = jnp.max(jnp.maximum(s_c, s_p), axis=1, keepdims=True)
                    p_c = jnp.exp2(s_c - m)
                    p_p = jnp.exp2(s_p - m)
                    l = jnp.sum(p_c + p_p, axis=1, keepdims=True)
                    p_prv.append(p_p.astype(BF16))
                else:
                    m = jnp.max(s_c, axis=1, keepdims=True)
                    p_c = jnp.exp2(s_c - m)
                    l = jnp.sum(p_c, axis=1, keepdims=True)
                    p_prv.append(None)
                p_cur.append(p_c.astype(BF16))
                m_blk.append(m)
                l_blk.append(l)
            o_blk = []
            for g in range(G):
                o = jnp.dot(p_cur[g], vg[g], preferred_element_type=F32)
                if has_prev[g]:
                    o = o + jnp.dot(p_prv[g], v_prev[g], preferred_element_type=F32)
                o_blk.append(o)

            a_out, m_out, l_out = [], [], []
            for g in range(G):
                if first:
                    a_out.append(o_blk[g])
                    m_out.append(jnp.broadcast_to(m_blk[g], (W, LANES)))
                    l_out.append(jnp.broadcast_to(l_blk[g], (W, LANES)))
                else:
                    m_new = jnp.maximum(m_old_g[g], m_blk[g])
                    e_old = jnp.exp2(m_old_g[g] - m_new)
                    e_blk = jnp.exp2(m_blk[g] - m_new)
                    a_out.append(a_old_g[g] * e_old + o_blk[g] * e_blk)
                    l_out.append(l_old_g[g] * e_old + l_blk[g] * e_blk)
                    m_out.append(m_new)

            for ref, vals in ((acc, a_out), (ms, m_out), (ls, l_out)):
                for a, st in enumerate(slab):
                    ref[pl.ds(st, G * run), :] = jnp.concatenate(
                        [v[a * run:(a + 1) * run, :] for v in vals], axis=0)
            return carry

        lax.fori_loop(0, s_len // (G * W), group, 0)

    def finish(r, carry):
        src = pl.ds(pl.multiple_of(r * per, per), per)
        o = acc[src, :] / ls[src, :]
        qs[pl.ds(r, per, stride=R), :] = _rms(o, aw_ref[...])
        return carry

    lax.fori_loop(0, R, finish, 0, unroll=4)
    out_ref[0] = qs[...].astype(BF16)


def _attention(proj3, cos_full, sin_signed, q_norm_w, k_norm_w, attn_norm_w):
    b, s, width = proj3.shape
    R = ATTN_RESIDUES
    nh = A_HEADS
    base = (2 * M_QK_WIDTH + 2 * M_V_WIDTH) // A_HEAD_DIM
    grouped = proj3.reshape(b, s // R, R * width)
    blocks_per_token = width // A_HEAD_DIM

    def residue_spec(r, off):
        col = r * blocks_per_token + base + off
        return pl.BlockSpec((1, s // R, A_HEAD_DIM), lambda i, h: (i, 0, col + h))

    qkv_specs = [residue_spec(r, off) for off in (0, nh, 2 * nh) for r in range(R)]
    return pl.pallas_call(
        _attn_kernel,
        grid=(b, nh),
        in_specs=qkv_specs + [
            pl.BlockSpec((s, A_HEAD_DIM), lambda i, h: (0, 0)),
            pl.BlockSpec((s, A_HEAD_DIM), lambda i, h: (0, 0)),
            pl.BlockSpec((1, A_HEAD_DIM), lambda i, h: (0, 0)),
            pl.BlockSpec((1, A_HEAD_DIM), lambda i, h: (0, 0)),
            pl.BlockSpec((1, A_HEAD_DIM), lambda i, h: (0, h)),
        ],
        out_specs=pl.BlockSpec((1, s, A_HEAD_DIM), lambda i, h: (i, 0, h)),
        out_shape=jax.ShapeDtypeStruct((b, s, A_WIDTH), BF16),
        scratch_shapes=[pltpu.VMEM((s, A_HEAD_DIM), F32) for _ in range(6)],
        compiler_params=pltpu.CompilerParams(
            dimension_semantics=("parallel", "parallel"), vmem_limit_bytes=VMEM_LIMIT),
        name="attention",
    )(*([grouped] * (3 * R)), cos_full, sin_signed, q_norm_w, k_norm_w, attn_norm_w)


def _out_proj_kernel(hm_ref, ha_ref, w_ref, x_ref, n2_ref, h_ref, u_ref):
    h1 = (x_ref[...]
          + jnp.dot(hm_ref[...], w_ref[:M_V_WIDTH, :], preferred_element_type=F32)
          + jnp.dot(ha_ref[...], w_ref[M_V_WIDTH:, :], preferred_element_type=F32))
    h_ref[...] = h1
    u_ref[...] = _rms(h1, n2_ref[...]).astype(BF16)


def _out_proj(hm, ha, w_out, x2d, n2, tm=512):
    t = x2d.shape[0]
    return pl.pallas_call(
        _out_proj_kernel,
        grid=(t // tm,),
        in_specs=[
            pl.BlockSpec((tm, M_V_WIDTH), lambda i: (i, 0)),
            pl.BlockSpec((tm, A_WIDTH), lambda i: (i, 0)),
            pl.BlockSpec((D_MODEL, D_MODEL), lambda i: (0, 0)),
            pl.BlockSpec((tm, D_MODEL), lambda i: (i, 0)),
            pl.BlockSpec((1, D_MODEL), lambda i: (0, 0)),
        ],
        out_specs=[
            pl.BlockSpec((tm, D_MODEL), lambda i: (i, 0)),
            pl.BlockSpec((tm, D_MODEL), lambda i: (i, 0)),
        ],
        out_shape=[
            jax.ShapeDtypeStruct((t, D_MODEL), F32),
            jax.ShapeDtypeStruct((t, D_MODEL), BF16),
        ],
        compiler_params=pltpu.CompilerParams(
            dimension_semantics=("parallel",), vmem_limit_bytes=VMEM_LIMIT),
        name="out_proj",
    )(hm, ha, w_out, x2d, n2)


def _ffn_act_kernel(u_ref, wg_ref, wu_ref, a_ref, wg_scr, wu_scr):
    @pl.when(pl.program_id(1) == 0)
    def _():
        wg_scr[...] = wg_ref[...].astype(BF16)
        wu_scr[...] = wu_ref[...].astype(BF16)

    u = u_ref[...]
    g = jnp.dot(u, wg_scr[...], preferred_element_type=F32)
    up = jnp.dot(u, wu_scr[...], preferred_element_type=F32)
    a_ref[...] = (g * _sigmoid(g) * up).astype(BF16)


def _ffn_act(u2, wg, wu, tm=1024, tf=512):
    t = u2.shape[0]
    return pl.pallas_call(
        _ffn_act_kernel,
        grid=(D_FF // tf, t // tm),
        in_specs=[
            pl.BlockSpec((tm, D_MODEL), lambda f, i: (i, 0)),
            pl.BlockSpec((D_MODEL, tf), lambda f, i: (0, f)),
            pl.BlockSpec((D_MODEL, tf), lambda f, i: (0, f)),
        ],
        out_specs=pl.BlockSpec((tm, tf), lambda f, i: (i, f)),
        out_shape=jax.ShapeDtypeStruct((t, D_FF), BF16),
        scratch_shapes=[pltpu.VMEM((D_MODEL, tf), BF16), pltpu.VMEM((D_MODEL, tf), BF16)],
        compiler_params=pltpu.CompilerParams(
            dimension_semantics=("parallel", "arbitrary"), vmem_limit_bytes=VMEM_LIMIT),
        name="ffn_act",
    )(u2, wg, wu)


def _ffn_down_kernel(a_ref, wd_ref, h_ref, o_ref):
    o_ref[...] = h_ref[...] + jnp.dot(a_ref[...], wd_ref[...], preferred_element_type=F32)


def _ffn_down(act, wd, h1, tm=512, tn=1024):
    t = act.shape[0]
    return pl.pallas_call(
        _ffn_down_kernel,
        grid=(D_MODEL // tn, t // tm),
        in_specs=[
            pl.BlockSpec((tm, D_FF), lambda j, i: (i, 0)),
            pl.BlockSpec((D_FF, tn), lambda j, i: (0, j)),
            pl.BlockSpec((tm, tn), lambda j, i: (i, j)),
        ],
        out_specs=pl.BlockSpec((tm, tn), lambda j, i: (i, j)),
        out_shape=jax.ShapeDtypeStruct((t, D_MODEL), F32),
        compiler_params=pltpu.CompilerParams(
            dimension_semantics=("parallel", "parallel"), vmem_limit_bytes=VMEM_LIMIT),
        name="ffn_down",
    )(act, wd, h1)


def _rope_tables(s):
    f32 = np.float32
    inv_freq = f32(ROPE_THETA) ** (-np.arange(0, A_HEAD_DIM, 2, dtype=f32) / f32(A_HEAD_DIM))
    ang = np.arange(s, dtype=f32)[:, None] * inv_freq[None, :]
    cos, sin = np.cos(ang).astype(f32), np.sin(ang).astype(f32)

    def grouped(t):
        return np.ascontiguousarray(
            t.reshape(s // ATTN_RESIDUES, ATTN_RESIDUES, A_HEAD_DIM).transpose(1, 0, 2)).reshape(
                s, A_HEAD_DIM)

    return (jnp.asarray(grouped(np.concatenate([cos, cos], axis=1))),
            jnp.asarray(grouped(np.concatenate([-sin, sin], axis=1))))


def _pad_lanes(a, width):
    return jnp.pad(a, ((0, 0), (0, width - a.shape[1])))


def kernel(x, norm1_w, w_in, conv_w, conv_b, igate_b, fgate_b, q_norm_w, k_norm_w,
           mlstm_norm_w, attn_norm_w, w_out, norm2_w, w_gate, w_up, w_down):
    b, s, d = x.shape
    depth = norm1_w.shape[0]
    cos_full, sin_signed = _rope_tables(s)
    h = x
    for l in range(depth):
        w_main, w_gates = _w_in_prep(w_in[l].T)
        w_main = w_main.reshape(PROJ_WIDTH, d)
        gate_b = jnp.concatenate(
            [_pad_lanes(igate_b[l][None, :], LANES), _pad_lanes(fgate_b[l][None, :], LANES)], axis=1)

        x2d = h.reshape(b * s, d)
        proj, gates = _in_proj(x2d, norm1_w[l][None, :], w_main, w_gates)
        proj3 = proj.reshape(b, s, PROJ_WIDTH)
        gates3 = gates.reshape(b, s, GATE_WIDTH)

        hm = _mlstm(proj3, gates3, conv_w[l], conv_b[l][None, :], gate_b,
                    mlstm_norm_w[l].reshape(1, M_V_WIDTH))
        ha = _attention(proj3, cos_full, sin_signed, q_norm_w[l][None, :], k_norm_w[l][None, :],
                        attn_norm_w[l].reshape(1, A_WIDTH))

        h1, u2 = _out_proj(hm.reshape(b * s, M_V_WIDTH), ha.reshape(b * s, A_WIDTH),
                           w_out[l].astype(BF16), x2d, norm2_w[l][None, :])
        act = _ffn_act(u2, w_gate[l], w_up[l])
        out = _ffn_down(act, w_down[l].astype(BF16), h1)
        h = out.reshape(b, s, d)
    return h
```

```python
import numpy as np
import jax
import jax.numpy as jnp
from jax import lax
from jax.experimental import pallas as pl
from jax.experimental.pallas import tpu as pltpu

D_MODEL = 2048
M_HEADS = 4
A_HEADS = 8
M_V_DIM = 256
M_QK_DIM = 128
A_HEAD_DIM = 128
CONV_WIDTH = 4
GATE_CAP = 15.0
ROPE_THETA = 10000.0
NORM_EPS = 1e-6
LOG2_E = 1.4426950408889634
D_FF = 5632
M_QK_WIDTH = M_HEADS * M_QK_DIM
M_V_WIDTH = M_HEADS * M_V_DIM
A_WIDTH = A_HEADS * A_HEAD_DIM
PROJ_WIDTH = 2 * M_QK_WIDTH + 2 * M_V_WIDTH + 3 * A_WIDTH
GATE_WIDTH = 256

LANES = 128
SUBLANES = 8
VMEM_LIMIT = 56 * 1024 * 1024

MLSTM_CHUNK = 256
ATTN_WINDOW_SUB = 128
ATTN_DILATIONS = (1, 4, 16)
ATTN_RESIDUES = 16
ATTN_GROUP = 8
ATTN_PREP_ROWS = 1024

BF16 = jnp.bfloat16
F32 = jnp.float32


def _sigmoid(x):
    return 1.0 / (1.0 + jnp.exp(-x))


def _rms(x, w):
    return x * lax.rsqrt(jnp.mean(x * x, axis=-1, keepdims=True) + NORM_EPS) * w


def _w_in_prep_kernel(left_ref, tail_ref, g_ref, main_ref, gate_ref):
    main_ref[0] = left_ref[...].astype(BF16)
    main_ref[1] = tail_ref[...].astype(BF16)

    @pl.when(pl.program_id(0) == 0)
    def _():
        g = g_ref[...]
        row = lax.broadcasted_iota(jnp.int32, g.shape, 0)
        zero = jnp.zeros_like(g)
        gate_ref[...] = jnp.zeros_like(gate_ref)
        gate_ref[0:2 * SUBLANES, :] = jnp.concatenate(
            [jnp.where(row < M_HEADS, g, 0.0), zero], axis=0).astype(BF16)
        gate_ref[LANES:LANES + 2 * SUBLANES, :] = jnp.concatenate(
            [jnp.where(row < M_HEADS, pltpu.roll(g, M_HEADS, 0), 0.0), zero], axis=0).astype(BF16)


def _w_in_prep(w_t, tr=256):
    n, k = w_t.shape
    g0 = 2 * M_QK_WIDTH + 2 * M_V_WIDTH
    half = PROJ_WIDTH // 2
    assert g0 == half and n == PROJ_WIDTH + 2 * M_HEADS
    return pl.pallas_call(
        _w_in_prep_kernel,
        grid=(half // tr,),
        in_specs=[
            pl.BlockSpec((tr, k), lambda i: (i, 0)),
            pl.BlockSpec((pl.Element(tr), pl.Element(k)),
                         lambda i: (pl.multiple_of(g0 + 2 * M_HEADS + i * tr, SUBLANES), 0)),
            pl.BlockSpec((pl.Element(2 * M_HEADS), pl.Element(k)), lambda i: (g0, 0)),
        ],
        out_specs=[
            pl.BlockSpec((2, tr, k), lambda i: (0, i, 0)),
            pl.BlockSpec((GATE_WIDTH, k), lambda i: (0, 0)),
        ],
        out_shape=[
            jax.ShapeDtypeStruct((2, half, k), BF16),
            jax.ShapeDtypeStruct((GATE_WIDTH, k), BF16),
        ],
        compiler_params=pltpu.CompilerParams(
            dimension_semantics=("arbitrary",), vmem_limit_bytes=VMEM_LIMIT),
        name="w_in_prep",
    )(w_t, w_t, w_t)


_NT = (((1,), (1,)), ((), ()))


def _in_proj_kernel(x_ref, n1_ref, w_ref, wg_ref, o_ref, og_ref, u_scr):
    @pl.when(pl.program_id(1) == 0)
    def _():
        u = _rms(x_ref[...], n1_ref[...]).astype(BF16)
        u_scr[...] = u
        og_ref[...] = lax.dot_general(u, wg_ref[...], _NT, preferred_element_type=F32)

    o_ref[...] = lax.dot_general(u_scr[...], w_ref[...], _NT, preferred_element_type=F32)


def _in_proj(x2d, n1, w_main, w_gate, tm=1024, tn=1024):
    t = x2d.shape[0]
    return pl.pallas_call(
        _in_proj_kernel,
        grid=(t // tm, PROJ_WIDTH // tn),
        in_specs=[
            pl.BlockSpec((tm, D_MODEL), lambda i, j: (i, 0)),
            pl.BlockSpec((1, D_MODEL), lambda i, j: (0, 0)),
            pl.BlockSpec((tn, D_MODEL), lambda i, j: (j, 0)),
            pl.BlockSpec((GATE_WIDTH, D_MODEL), lambda i, j: (0, 0)),
        ],
        out_specs=[
            pl.BlockSpec((tm, tn), lambda i, j: (i, j)),
            pl.BlockSpec((tm, GATE_WIDTH), lambda i, j: (i, 0)),
        ],
        out_shape=[
            jax.ShapeDtypeStruct((t, PROJ_WIDTH), F32),
            jax.ShapeDtypeStruct((t, GATE_WIDTH), F32),
        ],
        scratch_shapes=[pltpu.VMEM((tm, D_MODEL), BF16)],
        compiler_params=pltpu.CompilerParams(
            dimension_semantics=("parallel", "arbitrary"), vmem_limit_bytes=VMEM_LIMIT),
        name="in_proj",
    )(x2d, n1, w_main, w_gate)


def _mlstm_kernel(qk_ref, qkprev_ref, v_ref, o_ref, g_ref, cw_ref, cb_ref, gb_ref, nw_ref,
                  out_ref, xbuf, c_scr, m_scr):
    L = MLSTM_CHUNK
    c = pl.program_id(1)

    @pl.when(c == 0)
    def _():
        c_scr[...] = jnp.zeros_like(c_scr)
        m_scr[...] = jnp.zeros_like(m_scr)

    xbuf[0:SUBLANES, :] = jnp.where(c > 0, qkprev_ref[0], 0.0)
    xbuf[SUBLANES:, :] = qk_ref[0]
    y = cb_ref[...]
    for j in range(CONV_WIDTH):
        y = y + cw_ref[j:j + 1, :] * xbuf[pl.ds(SUBLANES - (CONV_WIDTH - 1) + j, L), :]
    qk = y * _sigmoid(y)

    gt = g_ref[0]
    i_pre = GATE_CAP * jnp.tanh((gt[:, :LANES] + gb_ref[:, :LANES]) / GATE_CAP)
    f_pre = GATE_CAP * jnp.tanh((gt[:, LANES:] + gb_ref[:, LANES:]) / GATE_CAP)
    log_f = jnp.minimum(f_pre, 0.0) - jnp.log1p(jnp.exp(-jnp.abs(f_pre)))

    row = lax.broadcasted_iota(jnp.int32, (L, L), 0)
    col = lax.broadcasted_iota(jnp.int32, (L, L), 1)
    causal = col <= row
    tril = jnp.where(causal, 1.0, 0.0).astype(BF16)
    hi = log_f.astype(BF16)
    r1 = log_f - hi.astype(F32)
    mid = r1.astype(BF16)
    lo = (r1 - mid.astype(F32)).astype(BF16)
    cum = (jnp.dot(tril, hi, preferred_element_type=F32)
           + jnp.dot(tril, mid, preferred_element_type=F32)
           + jnp.dot(tril, lo, preferred_element_type=F32))

    g_tot = cum[L - 1:L, :]
    a = g_tot - cum + i_pre
    m_loc = jnp.max(a, axis=0, keepdims=True)
    w_state = jnp.exp(a - m_loc)
    m_prev = m_scr[...]
    m_new = jnp.maximum(g_tot + m_prev, m_loc)
    s_old = jnp.exp(g_tot + m_prev - m_new)
    s_new = jnp.exp(m_loc - m_new)
    inter_log = cum + m_prev
    b_t = (i_pre - cum).T

    lane = lax.broadcasted_iota(jnp.int32, (L, LANES), 1)
    one_col = jnp.where(lane == 0, 1.0, 0.0).astype(BF16)
    vall = v_ref[0]
    oall = o_ref[0]
    heads = range(M_HEADS)
    vsl = [slice(h * M_V_DIM, (h + 1) * M_V_DIM) for h in heads]

    q = [(qk[:, h * M_QK_DIM:(h + 1) * M_QK_DIM] * (M_QK_DIM ** -0.5)).astype(BF16) for h in heads]
    kf = [qk[:, M_QK_WIDTH + h * M_QK_DIM:M_QK_WIDTH + (h + 1) * M_QK_DIM] for h in heads]
    v_aug = [jnp.concatenate([vall[:, vsl[h]].astype(BF16), one_col], axis=1) for h in heads]
    c_prev = [c_scr[h] for h in heads]
    s = [lax.dot_general(q[h], kf[h].astype(BF16), _NT, preferred_element_type=F32) for h in heads]
    inter = [jnp.dot(q[h], c_prev[h].astype(BF16), preferred_element_type=F32) for h in heads]
    kv = [jnp.dot((kf[h] * w_state[:, h:h + 1]).T.astype(BF16), v_aug[h],
                  preferred_element_type=F32) for h in heads]

    m_t, w_inter, s_qk = [], [], []
    for h in heads:
        d_log = jnp.where(causal, cum[:, h:h + 1] + b_t[h:h + 1, :], -jnp.inf)
        il = inter_log[:, h:h + 1]
        m = jnp.maximum(il, jnp.max(d_log, axis=1, keepdims=True))
        m_t.append(m)
        w_inter.append(jnp.exp(il - m))
        s_qk.append((s[h] * jnp.exp(d_log - m)).astype(BF16))

    intra = [jnp.dot(s_qk[h], v_aug[h], preferred_element_type=F32) for h in heads]

    for h in heads:
        tot = w_inter[h] * inter[h] + intra[h]
        num = tot[:, :M_V_DIM]
        den = tot[:, M_V_DIM:M_V_DIM + 1]
        hout = num / jnp.maximum(jnp.abs(den), jnp.exp(-m_t[h]))
        hn = _rms(hout, nw_ref[:, vsl[h]])
        out_ref[0, :, vsl[h]] = (hn * _sigmoid(oall[:, vsl[h]])).astype(BF16)
        c_scr[h] = s_old[:, h:h + 1] * c_prev[h] + s_new[:, h:h + 1] * kv[h]
    m_scr[...] = m_new


def _mlstm(proj3, gates3, conv_w, conv_b, gate_b, norm_w):
    b, s, _ = proj3.shape
    L = MLSTM_CHUNK
    qk_w = 2 * M_QK_WIDTH
    return pl.pallas_call(
        _mlstm_kernel,
        grid=(b, s // L),
        in_specs=[
            pl.BlockSpec((1, L, qk_w), lambda i, c: (i, c, 0)),
            pl.BlockSpec((1, SUBLANES, qk_w),
                         lambda i, c: (i, jnp.maximum(c * (L // SUBLANES) - 1, 0), 0)),
            pl.BlockSpec((1, L, M_V_WIDTH), lambda i, c: (i, c, qk_w // M_V_WIDTH)),
            pl.BlockSpec((1, L, M_V_WIDTH), lambda i, c: (i, c, qk_w // M_V_WIDTH + 1)),
            pl.BlockSpec((1, L, GATE_WIDTH), lambda i, c: (i, c, 0)),
            pl.BlockSpec((CONV_WIDTH, qk_w), lambda i, c: (0, 0)),
            pl.BlockSpec((1, qk_w), lambda i, c: (0, 0)),
            pl.BlockSpec((1, GATE_WIDTH), lambda i, c: (0, 0)),
            pl.BlockSpec((1, M_V_WIDTH), lambda i, c: (0, 0)),
        ],
        out_specs=pl.BlockSpec((1, L, M_V_WIDTH), lambda i, c: (i, c, 0)),
        out_shape=jax.ShapeDtypeStruct((b, s, M_V_WIDTH), BF16),
        scratch_shapes=[
            pltpu.VMEM((L + SUBLANES, qk_w), F32),
            pltpu.VMEM((M_HEADS, M_QK_DIM, M_V_DIM + LANES), F32),
            pltpu.VMEM((1, LANES), F32),
        ],
        compiler_params=pltpu.CompilerParams(
            dimension_semantics=("parallel", "arbitrary"), vmem_limit_bytes=VMEM_LIMIT),
        name="mlstm",
    )(proj3, proj3, proj3, proj3, gates3, conv_w, conv_b, gate_b, norm_w)


def _attn_kernel(q_ref, k_ref, v_ref, cos_ref, sin_ref, qw_ref, kw_ref, aw_ref, out_ref,
                 qs, ks, vs, acc, ms, ls):
    R = ATTN_RESIDUES
    s_len = out_ref.shape[1]
    per = s_len // R
    W = ATTN_WINDOW_SUB

    ones = jnp.ones((A_HEAD_DIM, LANES), BF16)
    row_out = lax.broadcasted_iota(jnp.int32, (W, W), 0)
    row_in = lax.broadcasted_iota(jnp.int32, (W, W), 1)
    rows_per_res = W // R
    perm = jnp.where((row_in % R) * rows_per_res + row_in // R == row_out, 1.0, 0.0).astype(BF16)
    slab = ATTN_PREP_ROWS
    n_chunks = slab // W

    def lane_ssq(x):
        sq = x * x
        hi = sq.astype(BF16)
        lo = (sq - hi.astype(F32)).astype(BF16)
        return (jnp.dot(hi, ones, preferred_element_type=F32)
                + jnp.dot(lo, ones, preferred_element_type=F32))

    def prep(ci, carry):
        base = pl.multiple_of(ci * slab, slab)
        rows = pl.ds(base, slab)
        xq, xk = q_ref[0, rows, :], k_ref[0, rows, :]
        ssq_q, ssq_k = lane_ssq(xq), lane_ssq(xk)
        cos, sin = cos_ref[rows, :], sin_ref[rows, :]

        def rope(x, ssq, w):
            xn = x * lax.rsqrt(ssq * (1.0 / A_HEAD_DIM) + NORM_EPS) * w
            return xn * cos + pltpu.roll(xn, A_HEAD_DIM // 2, 1) * sin

        srcs = (
            (rope(xq, ssq_q, qw_ref[...]) * (A_HEAD_DIM ** -0.5 * LOG2_E)).astype(BF16),
            rope(xk, ssq_k, kw_ref[...]).astype(BF16),
            v_ref[0, rows, :].astype(BF16))
        dst_base = pl.multiple_of(ci * (slab // R), slab // R)
        for src, dst in zip(srcs, (qs, ks, vs)):
            for j in range(n_chunks):
                out = jnp.dot(perm, src[j * W:(j + 1) * W, :], preferred_element_type=F32)
                for r in range(R):
                    dst[pl.ds(r * per + dst_base + j * rows_per_res, rows_per_res), :] = (
                        out[r * rows_per_res:(r + 1) * rows_per_res, :])
        return carry

    lax.fori_loop(0, s_len // slab, prep, 0)

    pos_q = lax.broadcasted_iota(jnp.int32, (W, W), 0)
    pos_k = lax.broadcasted_iota(jnp.int32, (W, W), 1)
    nt = (((1,), (1,)), ((), ()))
    G = ATTN_GROUP

    for pi, d in enumerate(ATTN_DILATIONS):
        run = SUBLANES * d
        nruns = W // run
        shift = run.bit_length() - 1

        def sub_pos(pos, run=run, nruns=nruns, shift=shift):
            return (pos & (run - 1)) * nruns + (pos >> shift)

        uq = sub_pos(pos_q)
        uk = sub_pos(pos_k)
        bias_cur = jnp.where(uk <= uq, 0.0, -jnp.inf).astype(F32)
        bias_prev = jnp.where(uk >= uq, 0.0, -jnp.inf).astype(F32)

        def group(gi, carry, d=d, run=run, nruns=nruns, bias_cur=bias_cur, bias_prev=bias_prev,
                  first=(pi == 0)):
            if d == 16:
                slab = [pl.multiple_of(gi * (G * W), G * W)]
            elif d == 4:
                slab = [pl.multiple_of(gi * per + a * (d * per), per) for a in range(nruns)]
            else:
                slab = [pl.multiple_of(a * per + gi * (G * run), G * run) for a in range(nruns)]

            def blocks(ref):
                slabs = [ref[pl.ds(st, G * run), :] for st in slab]
                return [jnp.concatenate([sl[g * run:(g + 1) * run, :] for sl in slabs], axis=0)
                        for g in range(G)]

            qg = [x.astype(BF16) for x in blocks(qs)]
            kg = [x.astype(BF16) for x in blocks(ks)]
            vg = [x.astype(BF16) for x in blocks(vs)]
            if d == 1:
                pst = [pl.multiple_of(a * per + jnp.maximum(gi * (G * run) - run, 0), run)
                       for a in range(nruns)]
                k_before = jnp.concatenate([ks[pl.ds(st, run), :] for st in pst], axis=0).astype(BF16)
                v_before = jnp.concatenate([vs[pl.ds(st, run), :] for st in pst], axis=0).astype(BF16)
                before_bias = jnp.where(gi > 0, 0.0, -jnp.inf)
            if not first:
                m_old_g, l_old_g, a_old_g = blocks(ms), blocks(ls), blocks(acc)

            if d == 16:
                has_prev = [g % 2 == 1 for g in range(G)]
            elif d == 4:
                has_prev = [g > 0 for g in range(G)]
            else:
                has_prev = [True] * G
            k_prev = [(k_before if d == 1 else None) if g == 0 else kg[g - 1] for g in range(G)]
            v_prev = [(v_before if d == 1 else None) if g == 0 else vg[g - 1] for g in range(G)]

            s_cur = [lax.dot_general(qg[g], kg[g], nt, preferred_element_type=F32) for g in range(G)]
            s_prv = [lax.dot_general(qg[g], k_prev[g], nt, preferred_element_type=F32)
                     if has_prev[g] else None for g in range(G)]
            p_cur, p_prv, m_blk, l_blk = [], [], [], []
            for g in range(G):
                s_c = s_cur[g] + bias_cur
                if has_prev[g]:
                    s_p = s_prv[g] + bias_prev
                    if g == 0:
                        s_p = s_p + before_bias
                    m = jnp.max(jnp.maximum(s_c, s_p), axis=1, keepdims=True)
                    p_c = jnp.exp2(s_c - m)
                    p_p = jnp.exp2(s_p - m)
                    l = jnp.sum(p_c + p_p, axis=1, keepdims=True)
                    p_prv.append(p_p.astype(BF16))
                else:
                    m = jnp.max(s_c, axis=1, keepdims=True)
                    p_c = jnp.exp2(s_c - m)
                    l = jnp.sum(p_c, axis=1, keepdims=True)
                    p_prv.append(None)
                p_cur.append(p_c.astype(BF16))
                m_blk.append(m)
                l_blk.append(l)
            o_blk = []
            for g in range(G):
                o = jnp.dot(p_cur[g], vg[g], preferred_element_type=F32)
                if has_prev[g]:
                    o = o + jnp.dot(p_prv[g], v_prev[g], preferred_element_type=F32)
                o_blk.append(o)

            a_out, m_out, l_out = [], [], []
            for g in range(G):
                if first:
                    a_out.append(o_blk[g])
                    m_out.append(jnp.broadcast_to(m_blk[g], (W, LANES)))
                    l_out.append(jnp.broadcast_to(l_blk[g], (W, LANES)))
                else:
                    m_new = jnp.maximum(m_old_g[g], m_blk[g])
                    e_old = jnp.exp2(m_old_g[g] - m_new)
                    e_blk = jnp.exp2(m_blk[g] - m_new)
                    a_out.append(a_old_g[g] * e_old + o_blk[g] * e_blk)
                    l_out.append(l_old_g[g] * e_old + l_blk[g] * e_blk)
                    m_out.append(m_new)

            for ref, vals in ((acc, a_out), (ms, m_out), (ls, l_out)):
                for a, st in enumerate(slab):
                    ref[pl.ds(st, G * run), :] = jnp.concatenate(
                        [v[a * run:(a + 1) * run, :] for v in vals], axis=0)
            return carry

        lax.fori_loop(0, s_len // (G * W), group, 0)

    def finish(r, carry):
        src = pl.ds(pl.multiple_of(r * per, per), per)
        o = acc[src, :] / ls[src, :]
        qs[pl.ds(r, per, stride=R), :] = _rms(o, aw_ref[...])
        return carry

    lax.fori_loop(0, R, finish, 0, unroll=4)
    out_ref[0] = qs[...].astype(BF16)


def _attention(proj3, cos_full, sin_signed, q_norm_w, k_norm_w, attn_norm_w):
    b, s, _ = proj3.shape
    base = (2 * M_QK_WIDTH + 2 * M_V_WIDTH) // A_HEAD_DIM
    nh = A_HEADS
    row_spec = lambda off: pl.BlockSpec((1, s, A_HEAD_DIM), lambda i, h: (i, 0, base + off + h))
    return pl.pallas_call(
        _attn_kernel,
        grid=(b, nh),
        in_specs=[
            row_spec(0), row_spec(nh), row_spec(2 * nh),
            pl.BlockSpec((s, A_HEAD_DIM), lambda i, h: (0, 0)),
            pl.BlockSpec((s, A_HEAD_DIM), lambda i, h: (0, 0)),
            pl.BlockSpec((1, A_HEAD_DIM), lambda i, h: (0, 0)),
            pl.BlockSpec((1, A_HEAD_DIM), lambda i, h: (0, 0)),
            pl.BlockSpec((1, A_HEAD_DIM), lambda i, h: (0, h)),
        ],
        out_specs=pl.BlockSpec((1, s, A_HEAD_DIM), lambda i, h: (i, 0, h)),
        out_shape=jax.ShapeDtypeStruct((b, s, A_WIDTH), BF16),
        scratch_shapes=[pltpu.VMEM((s, A_HEAD_DIM), F32) for _ in range(6)],
        compiler_params=pltpu.CompilerParams(
            dimension_semantics=("parallel", "parallel"), vmem_limit_bytes=VMEM_LIMIT),
        name="attention",
    )(proj3, proj3, proj3, cos_full, sin_signed, q_norm_w, k_norm_w, attn_norm_w)


def _out_proj_kernel(hm_ref, ha_ref, w_ref, x_ref, n2_ref, h_ref, u_ref):
    h1 = (x_ref[...]
          + jnp.dot(hm_ref[...], w_ref[:M_V_WIDTH, :], preferred_element_type=F32)
          + jnp.dot(ha_ref[...], w_ref[M_V_WIDTH:, :], preferred_element_type=F32))
    h_ref[...] = h1
    u_ref[...] = _rms(h1, n2_ref[...]).astype(BF16)


def _out_proj(hm, ha, w_out, x2d, n2, tm=512):
    t = x2d.shape[0]
    return pl.pallas_call(
        _out_proj_kernel,
        grid=(t // tm,),
        in_specs=[
            pl.BlockSpec((tm, M_V_WIDTH), lambda i: (i, 0)),
            pl.BlockSpec((tm, A_WIDTH), lambda i: (i, 0)),
            pl.BlockSpec((D_MODEL, D_MODEL), lambda i: (0, 0)),
            pl.BlockSpec((tm, D_MODEL), lambda i: (i, 0)),
            pl.BlockSpec((1, D_MODEL), lambda i: (0, 0)),
        ],
        out_specs=[
            pl.BlockSpec((tm, D_MODEL), lambda i: (i, 0)),
            pl.BlockSpec((tm, D_MODEL), lambda i: (i, 0)),
        ],
        out_shape=[
            jax.ShapeDtypeStruct((t, D_MODEL), F32),
            jax.ShapeDtypeStruct((t, D_MODEL), BF16),
        ],
        compiler_params=pltpu.CompilerParams(
            dimension_semantics=("parallel",), vmem_limit_bytes=VMEM_LIMIT),
        name="out_proj",
    )(hm, ha, w_out, x2d, n2)


def _ffn_act_kernel(u_ref, wg_ref, wu_ref, a_ref, wg_scr, wu_scr):
    @pl.when(pl.program_id(1) == 0)
    def _():
        wg_scr[...] = wg_ref[...].astype(BF16)
        wu_scr[...] = wu_ref[...].astype(BF16)

    u = u_ref[...]
    g = jnp.dot(u, wg_scr[...], preferred_element_type=F32)
    up = jnp.dot(u, wu_scr[...], preferred_element_type=F32)
    a_ref[...] = (g * _sigmoid(g) * up).astype(BF16)


def _ffn_act(u2, wg, wu, tm=1024, tf=512):
    t = u2.shape[0]
    return pl.pallas_call(
        _ffn_act_kernel,
        grid=(D_FF // tf, t // tm),
        in_specs=[
            pl.BlockSpec((tm, D_MODEL), lambda f, i: (i, 0)),
            pl.BlockSpec((D_MODEL, tf), lambda f, i: (0, f)),
            pl.BlockSpec((D_MODEL, tf), lambda f, i: (0, f)),
        ],
        out_specs=pl.BlockSpec((tm, tf), lambda f, i: (i, f)),
        out_shape=jax.ShapeDtypeStruct((t, D_FF), BF16),
        scratch_shapes=[pltpu.VMEM((D_MODEL, tf), BF16), pltpu.VMEM((D_MODEL, tf), BF16)],
        compiler_params=pltpu.CompilerParams(
            dimension_semantics=("parallel", "arbitrary"), vmem_limit_bytes=VMEM_LIMIT),
        name="ffn_act",
    )(u2, wg, wu)


def _ffn_down_kernel(a_ref, wd_ref, h_ref, o_ref):
    o_ref[...] = h_ref[...] + jnp.dot(a_ref[...], wd_ref[...], preferred_element_type=F32)


def _ffn_down(act, wd, h1, tm=512, tn=1024):
    t = act.shape[0]
    return pl.pallas_call(
        _ffn_down_kernel,
        grid=(D_MODEL // tn, t // tm),
        in_specs=[
            pl.BlockSpec((tm, D_FF), lambda j, i: (i, 0)),
            pl.BlockSpec((D_FF, tn), lambda j, i: (0, j)),
            pl.BlockSpec((tm, tn), lambda j, i: (i, j)),
        ],
        out_specs=pl.BlockSpec((tm, tn), lambda j, i: (i, j)),
        out_shape=jax.ShapeDtypeStruct((t, D_MODEL), F32),
        compiler_params=pltpu.CompilerParams(
            dimension_semantics=("parallel", "parallel"), vmem_limit_bytes=VMEM_LIMIT),
        name="ffn_down",
    )(act, wd, h1)


def _rope_tables(s):
    f32 = np.float32
    inv_freq = f32(ROPE_THETA) ** (-np.arange(0, A_HEAD_DIM, 2, dtype=f32) / f32(A_HEAD_DIM))
    ang = np.arange(s, dtype=f32)[:, None] * inv_freq[None, :]
    cos, sin = np.cos(ang).astype(f32), np.sin(ang).astype(f32)

    return (jnp.asarray(np.concatenate([cos, cos], axis=1)),
            jnp.asarray(np.concatenate([-sin, sin], axis=1)))


def _pad_lanes(a, width):
    return jnp.pad(a, ((0, 0), (0, width - a.shape[1])))


def kernel(x, norm1_w, w_in, conv_w, conv_b, igate_b, fgate_b, q_norm_w, k_norm_w,
           mlstm_norm_w, attn_norm_w, w_out, norm2_w, w_gate, w_up, w_down):
    b, s, d = x.shape
    depth = norm1_w.shape[0]
    cos_full, sin_signed = _rope_tables(s)
    h = x
    for l in range(depth):
        w_main, w_gates = _w_in_prep(w_in[l].T)
        w_main = w_main.reshape(PROJ_WIDTH, d)
        gate_b = jnp.concatenate(
            [_pad_lanes(igate_b[l][None, :], LANES), _pad_lanes(fgate_b[l][None, :], LANES)], axis=1)

        x2d = h.reshape(b * s, d)
        proj, gates = _in_proj(x2d, norm1_w[l][None, :], w_main, w_gates)
        proj3 = proj.reshape(b, s, PROJ_WIDTH)
        gates3 = gates.reshape(b, s, GATE_WIDTH)

        hm = _mlstm(proj3, gates3, conv_w[l], conv_b[l][None, :], gate_b,
                    mlstm_norm_w[l].reshape(1, M_V_WIDTH))
        ha = _attention(proj3, cos_full, sin_signed, q_norm_w[l][None, :], k_norm_w[l][None, :],
                        attn_norm_w[l].reshape(1, A_WIDTH))

        h1, u2 = _out_proj(hm.reshape(b * s, M_V_WIDTH), ha.reshape(b * s, A_WIDTH),
                           w_out[l].astype(BF16), x2d, norm2_w[l][None, :])
        act = _ffn_act(u2, w_gate[l], w_up[l])
        out = _ffn_down(act, w_down[l].astype(BF16), h1)
        h = out.reshape(b, s, d)
    return h
```

```python
import numpy as np
import jax
import jax.numpy as jnp
from jax import lax
from jax.experimental import pallas as pl
from jax.experimental.pallas import tpu as pltpu

D_MODEL = 2048
M_HEADS = 4
A_HEADS = 8
M_V_DIM = 256
M_QK_DIM = 128
A_HEAD_DIM = 128
CONV_WIDTH = 4
GATE_CAP = 15.0
ROPE_THETA = 10000.0
NORM_EPS = 1e-6
LOG2_E = 1.4426950408889634
D_FF = 5632
M_QK_WIDTH = M_HEADS * M_QK_DIM
M_V_WIDTH = M_HEADS * M_V_DIM
A_WIDTH = A_HEADS * A_HEAD_DIM
PROJ_WIDTH = 2 * M_QK_WIDTH + 2 * M_V_WIDTH + 3 * A_WIDTH
GATE_WIDTH = 128

LANES = 128
SUBLANES = 8
VMEM_LIMIT = 60 * 1024 * 1024

MLSTM_CHUNK = 256
ATTN_WINDOW_SUB = 128
ATTN_DILATIONS = (1, 4, 16)
ATTN_RESIDUES = 16
ATTN_GROUP = 8
ATTN_PREP_ROWS = 1024

BF16 = jnp.bfloat16
F32 = jnp.float32


def _sigmoid(x):
    return 1.0 / (1.0 + jnp.exp(-x))


def _rms(x, w):
    return x * lax.rsqrt(jnp.mean(x * x, axis=-1, keepdims=True) + NORM_EPS) * w


def _w_in_prep_kernel(left_ref, tail_ref, g_ref, main_ref, gate_ref):
    main_ref[0] = left_ref[...].astype(BF16)
    main_ref[1] = tail_ref[...].astype(BF16)

    @pl.when(pl.program_id(0) == 0)
    def _():
        g = g_ref[...]
        gate_ref[...] = jnp.zeros_like(gate_ref)
        gate_ref[0:2 * SUBLANES, :] = jnp.concatenate([g, jnp.zeros_like(g)], axis=0).astype(BF16)


def _w_in_prep(w_t, tr=256):
    n, k = w_t.shape
    g0 = 2 * M_QK_WIDTH + 2 * M_V_WIDTH
    half = PROJ_WIDTH // 2
    assert g0 == half and n == PROJ_WIDTH + 2 * M_HEADS
    return pl.pallas_call(
        _w_in_prep_kernel,
        grid=(half // tr,),
        in_specs=[
            pl.BlockSpec((tr, k), lambda i: (i, 0)),
            pl.BlockSpec((pl.Element(tr), pl.Element(k)),
                         lambda i: (pl.multiple_of(g0 + 2 * M_HEADS + i * tr, SUBLANES), 0)),
            pl.BlockSpec((pl.Element(2 * M_HEADS), pl.Element(k)), lambda i: (g0, 0)),
        ],
        out_specs=[
            pl.BlockSpec((2, tr, k), lambda i: (0, i, 0)),
            pl.BlockSpec((GATE_WIDTH, k), lambda i: (0, 0)),
        ],
        out_shape=[
            jax.ShapeDtypeStruct((2, half, k), BF16),
            jax.ShapeDtypeStruct((GATE_WIDTH, k), BF16),
        ],
        compiler_params=pltpu.CompilerParams(
            dimension_semantics=("arbitrary",), vmem_limit_bytes=VMEM_LIMIT),
        name="w_in_prep",
    )(w_t, w_t, w_t)


_NT = (((1,), (1,)), ((), ()))


def _in_proj_kernel(x_ref, n1_ref, w_ref, wg_ref, o_ref, og_ref, u_scr):
    @pl.when(pl.program_id(1) == 0)
    def _():
        u = _rms(x_ref[...], n1_ref[...]).astype(BF16)
        u_scr[...] = u
        og_ref[...] = lax.dot_general(u, wg_ref[...], _NT, preferred_element_type=F32)

    o_ref[...] = lax.dot_general(u_scr[...], w_ref[...], _NT, preferred_element_type=F32)


def _in_proj(x2d, n1, w_main, w_gate, tm=1024, tn=2048):
    t = x2d.shape[0]
    return pl.pallas_call(
        _in_proj_kernel,
        grid=(t // tm, PROJ_WIDTH // tn),
        in_specs=[
            pl.BlockSpec((tm, D_MODEL), lambda i, j: (i, 0)),
            pl.BlockSpec((1, D_MODEL), lambda i, j: (0, 0)),
            pl.BlockSpec((tn, D_MODEL), lambda i, j: (j, 0)),
            pl.BlockSpec((GATE_WIDTH, D_MODEL), lambda i, j: (0, 0)),
        ],
        out_specs=[
            pl.BlockSpec((tm, tn), lambda i, j: (i, j)),
            pl.BlockSpec((tm, GATE_WIDTH), lambda i, j: (i, 0)),
        ],
        out_shape=[
            jax.ShapeDtypeStruct((t, PROJ_WIDTH), F32),
            jax.ShapeDtypeStruct((t, GATE_WIDTH), F32),
        ],
        scratch_shapes=[pltpu.VMEM((tm, D_MODEL), BF16)],
        compiler_params=pltpu.CompilerParams(
            dimension_semantics=("parallel", "arbitrary"), vmem_limit_bytes=VMEM_LIMIT),
        name="in_proj",
    )(x2d, n1, w_main, w_gate)


def _mlstm_kernel(qk_ref, qkprev_ref, v_ref, o_ref, g_ref, cw_ref, cb_ref, gb_ref, nw_ref,
                  out_ref, xbuf, c_scr, m_scr):
    L = MLSTM_CHUNK
    c = pl.program_id(1)

    @pl.when(c == 0)
    def _():
        c_scr[...] = jnp.zeros_like(c_scr)
        m_scr[...] = jnp.zeros_like(m_scr)

    xbuf[0:SUBLANES, :] = jnp.where(c > 0, qkprev_ref[0], 0.0)
    xbuf[SUBLANES:, :] = qk_ref[0]
    y = cb_ref[...]
    for j in range(CONV_WIDTH):
        y = y + cw_ref[j:j + 1, :] * xbuf[pl.ds(SUBLANES - (CONV_WIDTH - 1) + j, L), :]
    qk = y * _sigmoid(y)

    capped = GATE_CAP * jnp.tanh((g_ref[0] + gb_ref[...]) / GATE_CAP)
    i_pre = capped
    f_pre = pltpu.roll(capped, LANES - M_HEADS, 1)
    log_f = jnp.minimum(f_pre, 0.0) - jnp.log1p(jnp.exp(-jnp.abs(f_pre)))

    row = lax.broadcasted_iota(jnp.int32, (L, L), 0)
    col = lax.broadcasted_iota(jnp.int32, (L, L), 1)
    causal = col <= row
    tril = jnp.where(causal, 1.0, 0.0).astype(BF16)
    hi = log_f.astype(BF16)
    r1 = log_f - hi.astype(F32)
    mid = r1.astype(BF16)
    lo = (r1 - mid.astype(F32)).astype(BF16)
    cum = (jnp.dot(tril, hi, preferred_element_type=F32)
           + jnp.dot(tril, mid, preferred_element_type=F32)
           + jnp.dot(tril, lo, preferred_element_type=F32))

    g_tot = cum[L - 1:L, :]
    a = g_tot - cum + i_pre
    m_loc = jnp.max(a, axis=0, keepdims=True)
    w_state = jnp.exp(a - m_loc)
    m_prev = m_scr[...]
    m_new = jnp.maximum(g_tot + m_prev, m_loc)
    s_old = jnp.exp(g_tot + m_prev - m_new)
    s_new = jnp.exp(m_loc - m_new)
    inter_log = cum + m_prev
    b_t = (i_pre - cum).T

    lane = lax.broadcasted_iota(jnp.int32, (L, LANES), 1)
    one_col = jnp.where(lane == 0, 1.0, 0.0).astype(BF16)
    vall = v_ref[0]
    oall = o_ref[0]
    heads = range(M_HEADS)
    vsl = [slice(h * M_V_DIM, (h + 1) * M_V_DIM) for h in heads]

    q = [(qk[:, h * M_QK_DIM:(h + 1) * M_QK_DIM] * (M_QK_DIM ** -0.5)).astype(BF16) for h in heads]
    kf = [qk[:, M_QK_WIDTH + h * M_QK_DIM:M_QK_WIDTH + (h + 1) * M_QK_DIM] for h in heads]
    v_aug = [jnp.concatenate([vall[:, vsl[h]].astype(BF16), one_col], axis=1) for h in heads]
    c_prev = [c_scr[h] for h in heads]
    s = [lax.dot_general(q[h], kf[h].astype(BF16), _NT, preferred_element_type=F32) for h in heads]
    inter = [jnp.dot(q[h], c_prev[h].astype(BF16), preferred_element_type=F32) for h in heads]
    kv = [jnp.dot((kf[h] * w_state[:, h:h + 1]).T.astype(BF16), v_aug[h],
                  preferred_element_type=F32) for h in heads]

    m_t, w_inter, s_qk = [], [], []
    for h in heads:
        d_log = jnp.where(causal, cum[:, h:h + 1] + b_t[h:h + 1, :], -jnp.inf)
        il = inter_log[:, h:h + 1]
        m = jnp.maximum(il, jnp.max(d_log, axis=1, keepdims=True))
        m_t.append(m)
        w_inter.append(jnp.exp(il - m))
        s_qk.append((s[h] * jnp.exp(d_log - m)).astype(BF16))

    intra = [jnp.dot(s_qk[h], v_aug[h], preferred_element_type=F32) for h in heads]

    for h in heads:
        tot = w_inter[h] * inter[h] + intra[h]
        num = tot[:, :M_V_DIM]
        den = tot[:, M_V_DIM:M_V_DIM + 1]
        hout = num / jnp.maximum(jnp.abs(den), jnp.exp(-m_t[h]))
        hn = _rms(hout, nw_ref[:, vsl[h]])
        out_ref[0, :, vsl[h]] = (hn * _sigmoid(oall[:, vsl[h]])).astype(BF16)
        c_scr[h] = s_old[:, h:h + 1] * c_prev[h] + s_new[:, h:h + 1] * kv[h]
    m_scr[...] = m_new


def _mlstm(proj3, gates3, conv_w, conv_b, gate_b, norm_w):
    b, s, _ = proj3.shape
    L = MLSTM_CHUNK
    qk_w = 2 * M_QK_WIDTH
    return pl.pallas_call(
        _mlstm_kernel,
        grid=(b, s // L),
        in_specs=[
            pl.BlockSpec((1, L, qk_w), lambda i, c: (i, c, 0)),
            pl.BlockSpec((1, SUBLANES, qk_w),
                         lambda i, c: (i, jnp.maximum(c * (L // SUBLANES) - 1, 0), 0)),
            pl.BlockSpec((1, L, M_V_WIDTH), lambda i, c: (i, c, qk_w // M_V_WIDTH)),
            pl.BlockSpec((1, L, M_V_WIDTH), lambda i, c: (i, c, qk_w // M_V_WIDTH + 1)),
            pl.BlockSpec((1, L, GATE_WIDTH), lambda i, c: (i, c, 0)),
            pl.BlockSpec((CONV_WIDTH, qk_w), lambda i, c: (0, 0)),
            pl.BlockSpec((1, qk_w), lambda i, c: (0, 0)),
            pl.BlockSpec((1, GATE_WIDTH), lambda i, c: (0, 0)),
            pl.BlockSpec((1, M_V_WIDTH), lambda i, c: (0, 0)),
        ],
        out_specs=pl.BlockSpec((1, L, M_V_WIDTH), lambda i, c: (i, c, 0)),
        out_shape=jax.ShapeDtypeStruct((b, s, M_V_WIDTH), BF16),
        scratch_shapes=[
            pltpu.VMEM((L + SUBLANES, qk_w), F32),
            pltpu.VMEM((M_HEADS, M_QK_DIM, M_V_DIM + LANES), F32),
            pltpu.VMEM((1, LANES), F32),
        ],
        compiler_params=pltpu.CompilerParams(
            dimension_semantics=("parallel", "arbitrary"), vmem_limit_bytes=VMEM_LIMIT),
        name="mlstm",
    )(proj3, proj3, proj3, proj3, gates3, conv_w, conv_b, gate_b, norm_w)


def _attn_kernel(q_ref, k_ref, v_ref, cos_ref, sin_ref, qw_ref, kw_ref, aw_ref, out_ref,
                 qs, ks, vs, acc, ms, ls):
    R = ATTN_RESIDUES
    s_len = out_ref.shape[1]
    per = s_len // R
    W = ATTN_WINDOW_SUB

    ones = jnp.ones((A_HEAD_DIM, LANES), BF16)
    row_out = lax.broadcasted_iota(jnp.int32, (W, W), 0)
    row_in = lax.broadcasted_iota(jnp.int32, (W, W), 1)
    rows_per_res = W // R
    perm = jnp.where((row_in % R) * rows_per_res + row_in // R == row_out, 1.0, 0.0).astype(BF16)
    slab = ATTN_PREP_ROWS
    n_chunks = slab // W

    def lane_ssq(x):
        sq = x * x
        hi = sq.astype(BF16)
        lo = (sq - hi.astype(F32)).astype(BF16)
        return (jnp.dot(hi, ones, preferred_element_type=F32)
                + jnp.dot(lo, ones, preferred_element_type=F32))

    def prep(ci, carry):
        base = pl.multiple_of(ci * slab, slab)
        rows = pl.ds(base, slab)
        xq, xk = q_ref[0, rows, :], k_ref[0, rows, :]
        ssq_q, ssq_k = lane_ssq(xq), lane_ssq(xk)
        cos, sin = cos_ref[rows, :], sin_ref[rows, :]

        def rope(x, ssq, w):
            xn = x * lax.rsqrt(ssq * (1.0 / A_HEAD_DIM) + NORM_EPS) * w
            return xn * cos + pltpu.roll(xn, A_HEAD_DIM // 2, 1) * sin

        srcs = (
            (rope(xq, ssq_q, qw_ref[...]) * (A_HEAD_DIM ** -0.5 * LOG2_E)).astype(BF16),
            rope(xk, ssq_k, kw_ref[...]).astype(BF16),
            v_ref[0, rows, :].astype(BF16))
        dst_base = pl.multiple_of(ci * (slab // R), slab // R)
        for src, dst in zip(srcs, (qs, ks, vs)):
            for j in range(n_chunks):
                out = jnp.dot(perm, src[j * W:(j + 1) * W, :], preferred_element_type=F32)
                for r in range(R):
                    dst[pl.ds(r * per + dst_base + j * rows_per_res, rows_per_res), :] = (
                        out[r * rows_per_res:(r + 1) * rows_per_res, :])
        return carry

    lax.fori_loop(0, s_len // slab, prep, 0)

    pos_q = lax.broadcasted_iota(jnp.int32, (W, W), 0)
    pos_k = lax.broadcasted_iota(jnp.int32, (W, W), 1)
    nt = (((1,), (1,)), ((), ()))
    G = ATTN_GROUP

    for pi, d in enumerate(ATTN_DILATIONS):
        run = SUBLANES * d
        nruns = W // run
        shift = run.bit_length() - 1

        def sub_pos(pos, run=run, nruns=nruns, shift=shift):
            return (pos & (run - 1)) * nruns + (pos >> shift)

        uq = sub_pos(pos_q)
        uk = sub_pos(pos_k)
        bias_cur = jnp.where(uk <= uq, 0.0, -jnp.inf).astype(F32)
        bias_prev = jnp.where(uk >= uq, 0.0, -jnp.inf).astype(F32)

        def group(gi, carry, d=d, run=run, nruns=nruns, bias_cur=bias_cur, bias_prev=bias_prev,
                  first=(pi == 0)):
            if d == 16:
                slab = [pl.multiple_of(gi * (G * W), G * W)]
            elif d == 4:
                slab = [pl.multiple_of(gi * per + a * (d * per), per) for a in range(nruns)]
            else:
                slab = [pl.multiple_of(a * per + gi * (G * run), G * run) for a in range(nruns)]

            def blocks(ref):
                slabs = [ref[pl.ds(st, G * run), :] for st in slab]
                return [jnp.concatenate([sl[g * run:(g + 1) * run, :] for sl in slabs], axis=0)
                        for g in range(G)]

            qg = [x.astype(BF16) for x in blocks(qs)]
            kg = [x.astype(BF16) for x in blocks(ks)]
            vg = [x.astype(BF16) for x in blocks(vs)]
            if d == 1:
                pst = [pl.multiple_of(a * per + jnp.maximum(gi * (G * run) - run, 0), run)
                       for a in range(nruns)]
                k_before = jnp.concatenate([ks[pl.ds(st, run), :] for st in pst], axis=0).astype(BF16)
                v_before = jnp.concatenate([vs[pl.ds(st, run), :] for st in pst], axis=0).astype(BF16)
                before_bias = jnp.where(gi > 0, 0.0, -jnp.inf)
            if not first:
                m_old_g, l_old_g, a_old_g = blocks(ms), blocks(ls), blocks(acc)

            if d == 16:
                has_prev = [g % 2 == 1 for g in range(G)]
            elif d == 4:
                has_prev = [g > 0 for g in range(G)]
            else:
                has_prev = [True] * G
            k_prev = [(k_before if d == 1 else None) if g == 0 else kg[g - 1] for g in range(G)]
            v_prev = [(v_before if d == 1 else None) if g == 0 else vg[g - 1] for g in range(G)]

            s_cur = [lax.dot_general(qg[g], kg[g], nt, preferred_element_type=F32) for g in range(G)]
            s_prv = [lax.dot_general(qg[g], k_prev[g], nt, preferred_element_type=F32)
                     if has_prev[g] else None for g in range(G)]
            p_cur, p_prv, m_blk, l_blk = [], [], [], []
            for g in range(G):
                s_c = s_cur[g] + bias_cur
                if has_prev[g]:
                    s_p = s_prv[g] + bias_prev
                    if g == 0:
                        s_p = s_p + before_bias
                    m = jnp.max(jnp.maximum(s_c, s_p), axis=1, keepdims=True)
                    p_c = jnp.exp2(s_c - m)
                    p_p = jnp.exp2(s_p - m)
                    l = jnp.sum(p_c + p_p, axis=1, keepdims=True)
                    p_prv.append(p_p.astype(BF16))
                else:
                    m = jnp.max(s_c, axis=1, keepdims=True)
                    p_c = jnp.exp2(s_c - m)
                    l = jnp.sum(p_c, axis=1, keepdims=True)
                    p_prv.append(None)
                p_cur.append(p_c.astype(BF16))
                m_blk.append(m)
                l_blk.append(l)
            o_blk = []
            for g in range(G):
                o = jnp.dot(p_cur[g], vg[g], preferred_element_type=F32)
                if has_prev[g]:
                    o = o + jnp.dot(p_prv[g], v_prev[g], preferred_element_type=F32)
                o_blk.append(o)

            a_out, m_out, l_out = [], [], []
            for g in range(G):
                if first:
                    a_out.append(o_blk[g])
                    m_out.append(jnp.broadcast_to(m_blk[g], (W, LANES)))
                    l_out.append(jnp.broadcast_to(l_blk[g], (W, LANES)))
                else:
                    m_new = jnp.maximum(m_old_g[g], m_blk[g])
                    e_old = jnp.exp2(m_old_g[g] - m_new)
                    e_blk = jnp.exp2(m_blk[g] - m_new)
                    a_out.append(a_old_g[g] * e_old + o_blk[g] * e_blk)
                    l_out.append(l_old_g[g] * e_old + l_blk[g] * e_blk)
                    m_out.append(m_new)

            for ref, vals in ((acc, a_out), (ms, m_out), (ls, l_out)):
                for a, st in enumerate(slab):
                    ref[pl.ds(st, G * run), :] = jnp.concatenate(
                        [v[a * run:(a + 1) * run, :] for v in vals], axis=0)
            return carry

        lax.fori_loop(0, s_len // (G * W), group, 0)

    def finish(r, carry):
        src = pl.ds(pl.multiple_of(r * per, per), per)
        o = acc[src, :] / ls[src, :]
        qs[pl.ds(r, per, stride=R), :] = _rms(o, aw_ref[...])
        return carry

    lax.fori_loop(0, R, finish, 0, unroll=4)
    out_ref[0] = qs[...].astype(BF16)


def _attention(proj3, cos_full, sin_signed, q_norm_w, k_norm_w, attn_norm_w):
    b, s, _ = proj3.shape
    base = (2 * M_QK_WIDTH + 2 * M_V_WIDTH) // A_HEAD_DIM
    nh = A_HEADS
    row_spec = lambda off: pl.BlockSpec((1, s, A_HEAD_DIM), lambda i, h: (i, 0, base + off + h))
    return pl.pallas_call(
        _attn_kernel,
        grid=(b, nh),
        in_specs=[
            row_spec(0), row_spec(nh), row_spec(2 * nh),
            pl.BlockSpec((s, A_HEAD_DIM), lambda i, h: (0, 0)),
            pl.BlockSpec((s, A_HEAD_DIM), lambda i, h: (0, 0)),
            pl.BlockSpec((1, A_HEAD_DIM), lambda i, h: (0, 0)),
            pl.BlockSpec((1, A_HEAD_DIM), lambda i, h: (0, 0)),
            pl.BlockSpec((1, A_HEAD_DIM), lambda i, h: (0, h)),
        ],
        out_specs=pl.BlockSpec((1, s, A_HEAD_DIM), lambda i, h: (i, 0, h)),
        out_shape=jax.ShapeDtypeStruct((b, s, A_WIDTH), BF16),
        scratch_shapes=[pltpu.VMEM((s, A_HEAD_DIM), F32) for _ in range(6)],
        compiler_params=pltpu.CompilerParams(
            dimension_semantics=("parallel", "parallel"), vmem_limit_bytes=VMEM_LIMIT),
        name="attention",
    )(proj3, proj3, proj3, cos_full, sin_signed, q_norm_w, k_norm_w, attn_norm_w)


def _out_proj_kernel(hm_ref, ha_ref, w_ref, x_ref, n2_ref, h_ref, u_ref):
    h1 = (x_ref[...]
          + jnp.dot(hm_ref[...], w_ref[:M_V_WIDTH, :], preferred_element_type=F32)
          + jnp.dot(ha_ref[...], w_ref[M_V_WIDTH:, :], preferred_element_type=F32))
    h_ref[...] = h1
    u_ref[...] = _rms(h1, n2_ref[...]).astype(BF16)


def _out_proj(hm, ha, w_out, x2d, n2, tm=512):
    t = x2d.shape[0]
    return pl.pallas_call(
        _out_proj_kernel,
        grid=(t // tm,),
        in_specs=[
            pl.BlockSpec((tm, M_V_WIDTH), lambda i: (i, 0)),
            pl.BlockSpec((tm, A_WIDTH), lambda i: (i, 0)),
            pl.BlockSpec((D_MODEL, D_MODEL), lambda i: (0, 0)),
            pl.BlockSpec((tm, D_MODEL), lambda i: (i, 0)),
            pl.BlockSpec((1, D_MODEL), lambda i: (0, 0)),
        ],
        out_specs=[
            pl.BlockSpec((tm, D_MODEL), lambda i: (i, 0)),
            pl.BlockSpec((tm, D_MODEL), lambda i: (i, 0)),
        ],
        out_shape=[
            jax.ShapeDtypeStruct((t, D_MODEL), F32),
            jax.ShapeDtypeStruct((t, D_MODEL), BF16),
        ],
        compiler_params=pltpu.CompilerParams(
            dimension_semantics=("parallel",), vmem_limit_bytes=VMEM_LIMIT),
        name="out_proj",
    )(hm, ha, w_out, x2d, n2)


def _ffn_act_kernel(u_ref, wg_ref, wu_ref, a_ref, wg_scr, wu_scr):
    @pl.when(pl.program_id(1) == 0)
    def _():
        wg_scr[...] = wg_ref[...].astype(BF16)
        wu_scr[...] = wu_ref[...].astype(BF16)

    u = u_ref[...]
    g = jnp.dot(u, wg_scr[...], preferred_element_type=F32)
    up = jnp.dot(u, wu_scr[...], preferred_element_type=F32)
    a_ref[...] = (g * _sigmoid(g) * up).astype(BF16)


def _ffn_act(u2, wg, wu, tm=1024, tf=512):
    t = u2.shape[0]
    return pl.pallas_call(
        _ffn_act_kernel,
        grid=(D_FF // tf, t // tm),
        in_specs=[
            pl.BlockSpec((tm, D_MODEL), lambda f, i: (i, 0)),
            pl.BlockSpec((D_MODEL, tf), lambda f, i: (0, f)),
            pl.BlockSpec((D_MODEL, tf), lambda f, i: (0, f)),
        ],
        out_specs=pl.BlockSpec((tm, tf), lambda f, i: (i, f)),
        out_shape=jax.ShapeDtypeStruct((t, D_FF), BF16),
        scratch_shapes=[pltpu.VMEM((D_MODEL, tf), BF16), pltpu.VMEM((D_MODEL, tf), BF16)],
        compiler_params=pltpu.CompilerParams(
            dimension_semantics=("parallel", "arbitrary"), vmem_limit_bytes=VMEM_LIMIT),
        name="ffn_act",
    )(u2, wg, wu)


def _ffn_down_kernel(a_ref, wd_ref, h_ref, o_ref):
    o_ref[...] = h_ref[...] + jnp.dot(a_ref[...], wd_ref[...], preferred_element_type=F32)


def _ffn_down(act, wd, h1, tm=512, tn=1024):
    t = act.shape[0]
    return pl.pallas_call(
        _ffn_down_kernel,
        grid=(D_MODEL // tn, t // tm),
        in_specs=[
            pl.BlockSpec((tm, D_FF), lambda j, i: (i, 0)),
            pl.BlockSpec((D_FF, tn), lambda j, i: (0, j)),
            pl.BlockSpec((tm, tn), lambda j, i: (i, j)),
        ],
        out_specs=pl.BlockSpec((tm, tn), lambda j, i: (i, j)),
        out_shape=jax.ShapeDtypeStruct((t, D_MODEL), F32),
        compiler_params=pltpu.CompilerParams(
            dimension_semantics=("parallel", "parallel"), vmem_limit_bytes=VMEM_LIMIT),
        name="ffn_down",
    )(act, wd, h1)


def _rope_tables(s):
    f32 = np.float32
    inv_freq = f32(ROPE_THETA) ** (-np.arange(0, A_HEAD_DIM, 2, dtype=f32) / f32(A_HEAD_DIM))
    ang = np.arange(s, dtype=f32)[:, None] * inv_freq[None, :]
    cos, sin = np.cos(ang).astype(f32), np.sin(ang).astype(f32)

    return (jnp.asarray(np.concatenate([cos, cos], axis=1)),
            jnp.asarray(np.concatenate([-sin, sin], axis=1)))


def _pad_lanes(a, width):
    return jnp.pad(a, ((0, 0), (0, width - a.shape[1])))


def kernel(x, norm1_w, w_in, conv_w, conv_b, igate_b, fgate_b, q_norm_w, k_norm_w,
           mlstm_norm_w, attn_norm_w, w_out, norm2_w, w_gate, w_up, w_down):
    b, s, d = x.shape
    depth = norm1_w.shape[0]
    cos_full, sin_signed = _rope_tables(s)
    h = x
    for l in range(depth):
        w_main, w_gates = _w_in_prep(w_in[l].T)
        w_main = w_main.reshape(PROJ_WIDTH, d)
        gate_b = _pad_lanes(jnp.concatenate([igate_b[l], fgate_b[l]])[None, :], GATE_WIDTH)

        x2d = h.reshape(b * s, d)
        proj, gates = _in_proj(x2d, norm1_w[l][None, :], w_main, w_gates)
        proj3 = proj.reshape(b, s, PROJ_WIDTH)
        gates3 = gates.reshape(b, s, GATE_WIDTH)

        hm = _mlstm(proj3, gates3, conv_w[l], conv_b[l][None, :], gate_b,
                    mlstm_norm_w[l].reshape(1, M_V_WIDTH))
        ha = _attention(proj3, cos_full, sin_signed, q_norm_w[l][None, :], k_norm_w[l][None, :],
                        attn_norm_w[l].reshape(1, A_WIDTH))

        h1, u2 = _out_proj(hm.reshape(b * s, M_V_WIDTH), ha.reshape(b * s, A_WIDTH),
                           w_out[l].astype(BF16), x2d, norm2_w[l][None, :])
        act = _ffn_act(u2, w_gate[l], w_up[l])
        out = _ffn_down(act, w_down[l].astype(BF16), h1)
        h = out.reshape(b, s, d)
    return h
```

```python
import numpy as np
import jax
import jax.numpy as jnp
from jax import lax
from jax.experimental import pallas as pl
from jax.experimental.pallas import tpu as pltpu

D_MODEL = 2048
M_HEADS = 4
A_HEADS = 8
M_V_DIM = 256
M_QK_DIM = 128
A_HEAD_DIM = 128
CONV_WIDTH = 4
GATE_CAP = 15.0
ROPE_THETA = 10000.0
NORM_EPS = 1e-6
LOG2_E = 1.4426950408889634
D_FF = 5632
M_QK_WIDTH = M_HEADS * M_QK_DIM
M_V_WIDTH = M_HEADS * M_V_DIM
A_WIDTH = A_HEADS * A_HEAD_DIM
PROJ_WIDTH = 2 * M_QK_WIDTH + 2 * M_V_WIDTH + 3 * A_WIDTH
GATE_WIDTH = 256

LANES = 128
SUBLANES = 8
VMEM_LIMIT = 56 * 1024 * 1024
VMEM_LIMIT_WIDE = 60 * 1024 * 1024

MLSTM_CHUNK = 256
ATTN_WINDOW_SUB = 128
ATTN_DILATIONS = (1, 4, 16)
ATTN_RESIDUES = 16
ATTN_GROUP = 16
ATTN_PREP_ROWS = 1024

BF16 = jnp.bfloat16
F32 = jnp.float32


def _sigmoid(x):
    return 1.0 / (1.0 + jnp.exp(-x))


def _rms(x, w):
    return x * lax.rsqrt(jnp.mean(x * x, axis=-1, keepdims=True) + NORM_EPS) * w


def _w_in_prep_kernel(left_ref, tail_ref, g_ref, main_ref, gate_ref):
    main_ref[0] = left_ref[...].astype(BF16)
    main_ref[1] = tail_ref[...].astype(BF16)

    @pl.when(pl.program_id(0) == 0)
    def _():
        g = g_ref[...]
        gate_ref[...] = jnp.zeros_like(gate_ref)
        row = lax.broadcasted_iota(jnp.int32, g.shape, 0)
        zero = jnp.zeros_like(g)
        gate_ref[0:2 * SUBLANES, :] = jnp.concatenate(
            [jnp.where(row < M_HEADS, g, 0.0), zero], axis=0).astype(BF16)
        gate_ref[LANES:LANES + 2 * SUBLANES, :] = jnp.concatenate(
            [jnp.where(row < M_HEADS, pltpu.roll(g, M_HEADS, 0), 0.0), zero], axis=0).astype(BF16)


def _w_in_prep(w_t, tr=256):
    n, k = w_t.shape
    g0 = 2 * M_QK_WIDTH + 2 * M_V_WIDTH
    half = PROJ_WIDTH // 2
    assert g0 == half and n == PROJ_WIDTH + 2 * M_HEADS
    return pl.pallas_call(
        _w_in_prep_kernel,
        grid=(half // tr,),
        in_specs=[
            pl.BlockSpec((tr, k), lambda i: (i, 0)),
            pl.BlockSpec((pl.Element(tr), pl.Element(k)),
                         lambda i: (pl.multiple_of(g0 + 2 * M_HEADS + i * tr, SUBLANES), 0)),
            pl.BlockSpec((pl.Element(2 * M_HEADS), pl.Element(k)), lambda i: (g0, 0)),
        ],
        out_specs=[
            pl.BlockSpec((2, tr, k), lambda i: (0, i, 0)),
            pl.BlockSpec((GATE_WIDTH, k), lambda i: (0, 0)),
        ],
        out_shape=[
            jax.ShapeDtypeStruct((2, half, k), BF16),
            jax.ShapeDtypeStruct((GATE_WIDTH, k), BF16),
        ],
        compiler_params=pltpu.CompilerParams(
            dimension_semantics=("arbitrary",), vmem_limit_bytes=VMEM_LIMIT),
        name="w_in_prep",
    )(w_t, w_t, w_t)


_NT = (((1,), (1,)), ((), ()))


def _in_proj_kernel(x_ref, n1_ref, w_ref, wg_ref, o_ref, og_ref, u_scr):
    @pl.when(pl.program_id(1) == 0)
    def _():
        u = _rms(x_ref[...], n1_ref[...]).astype(BF16)
        u_scr[...] = u
        og_ref[...] = lax.dot_general(u, wg_ref[...], _NT, preferred_element_type=F32)

    o_ref[...] = lax.dot_general(u_scr[...], w_ref[...], _NT, preferred_element_type=F32)


def _in_proj(x2d, n1, w_main, w_gate, tm=1024, tn=2048):
    t = x2d.shape[0]
    return pl.pallas_call(
        _in_proj_kernel,
        grid=(t // tm, PROJ_WIDTH // tn),
        in_specs=[
            pl.BlockSpec((tm, D_MODEL), lambda i, j: (i, 0)),
            pl.BlockSpec((1, D_MODEL), lambda i, j: (0, 0)),
            pl.BlockSpec((tn, D_MODEL), lambda i, j: (j, 0)),
            pl.BlockSpec((GATE_WIDTH, D_MODEL), lambda i, j: (0, 0)),
        ],
        out_specs=[
            pl.BlockSpec((tm, tn), lambda i, j: (i, j)),
            pl.BlockSpec((tm, GATE_WIDTH), lambda i, j: (i, 0)),
        ],
        out_shape=[
            jax.ShapeDtypeStruct((t, PROJ_WIDTH), F32),
            jax.ShapeDtypeStruct((t, GATE_WIDTH), F32),
        ],
        scratch_shapes=[pltpu.VMEM((tm, D_MODEL), BF16)],
        compiler_params=pltpu.CompilerParams(
            dimension_semantics=("parallel", "arbitrary"), vmem_limit_bytes=VMEM_LIMIT_WIDE),
        name="in_proj",
    )(x2d, n1, w_main, w_gate)


def _mlstm_kernel(qk_ref, qkprev_ref, v_ref, o_ref, g_ref, cw_ref, cb_ref, gb_ref, nw_ref,
                  out_ref, xbuf, c_scr, m_scr):
    L = MLSTM_CHUNK
    c = pl.program_id(1)

    @pl.when(c == 0)
    def _():
        c_scr[...] = jnp.zeros_like(c_scr)
        m_scr[...] = jnp.zeros_like(m_scr)

    xbuf[0:SUBLANES, :] = jnp.where(c > 0, qkprev_ref[0], 0.0)
    xbuf[SUBLANES:, :] = qk_ref[0]
    y = cb_ref[...]
    for j in range(CONV_WIDTH):
        y = y + cw_ref[j:j + 1, :] * xbuf[pl.ds(SUBLANES - (CONV_WIDTH - 1) + j, L), :]
    qk = y * _sigmoid(y)

    gt = g_ref[0]
    i_pre = GATE_CAP * jnp.tanh((gt[:, :LANES] + gb_ref[:, :LANES]) / GATE_CAP)
    f_pre = GATE_CAP * jnp.tanh((gt[:, LANES:] + gb_ref[:, LANES:]) / GATE_CAP)
    log_f = jnp.minimum(f_pre, 0.0) - jnp.log1p(jnp.exp(-jnp.abs(f_pre)))

    row = lax.broadcasted_iota(jnp.int32, (L, L), 0)
    col = lax.broadcasted_iota(jnp.int32, (L, L), 1)
    causal = col <= row
    tril = jnp.where(causal, 1.0, 0.0).astype(BF16)
    hi = log_f.astype(BF16)
    r1 = log_f - hi.astype(F32)
    mid = r1.astype(BF16)
    lo = (r1 - mid.astype(F32)).astype(BF16)
    cum = (jnp.dot(tril, hi, preferred_element_type=F32)
           + jnp.dot(tril, mid, preferred_element_type=F32)
           + jnp.dot(tril, lo, preferred_element_type=F32))

    g_tot = cum[L - 1:L, :]
    a = g_tot - cum + i_pre
    m_loc = jnp.max(a, axis=0, keepdims=True)
    w_state = jnp.exp(a - m_loc)
    m_prev = m_scr[...]
    m_new = jnp.maximum(g_tot + m_prev, m_loc)
    s_old = jnp.exp(g_tot + m_prev - m_new)
    s_new = jnp.exp(m_loc - m_new)
    inter_log = cum + m_prev
    b_t = (i_pre - cum).T

    lane = lax.broadcasted_iota(jnp.int32, (L, LANES), 1)
    one_col = jnp.where(lane == 0, 1.0, 0.0).astype(BF16)
    vall = v_ref[0]
    oall = o_ref[0]
    heads = range(M_HEADS)
    vsl = [slice(h * M_V_DIM, (h + 1) * M_V_DIM) for h in heads]

    q = [(qk[:, h * M_QK_DIM:(h + 1) * M_QK_DIM] * (M_QK_DIM ** -0.5)).astype(BF16) for h in heads]
    kf = [qk[:, M_QK_WIDTH + h * M_QK_DIM:M_QK_WIDTH + (h + 1) * M_QK_DIM] for h in heads]
    v_aug = [jnp.concatenate([vall[:, vsl[h]].astype(BF16), one_col], axis=1) for h in heads]
    c_prev = [c_scr[h] for h in heads]
    s = [lax.dot_general(q[h], kf[h].astype(BF16), _NT, preferred_element_type=F32) for h in heads]
    inter = [jnp.dot(q[h], c_prev[h].astype(BF16), preferred_element_type=F32) for h in heads]
    kv = [jnp.dot((kf[h] * w_state[:, h:h + 1]).T.astype(BF16), v_aug[h],
                  preferred_element_type=F32) for h in heads]

    m_t, w_inter, s_qk = [], [], []
    for h in heads:
        d_log = jnp.where(causal, cum[:, h:h + 1] + b_t[h:h + 1, :], -jnp.inf)
        il = inter_log[:, h:h + 1]
        m = jnp.maximum(il, jnp.max(d_log, axis=1, keepdims=True))
        m_t.append(m)
        w_inter.append(jnp.exp(il - m))
        s_qk.append((s[h] * jnp.exp(d_log - m)).astype(BF16))

    intra = [jnp.dot(s_qk[h], v_aug[h], preferred_element_type=F32) for h in heads]

    for h in heads:
        tot = w_inter[h] * inter[h] + intra[h]
        num = tot[:, :M_V_DIM]
        den = tot[:, M_V_DIM:M_V_DIM + 1]
        hout = num / jnp.maximum(jnp.abs(den), jnp.exp(-m_t[h]))
        hn = _rms(hout, nw_ref[:, vsl[h]])
        out_ref[0, :, vsl[h]] = (hn * _sigmoid(oall[:, vsl[h]])).astype(BF16)
        c_scr[h] = s_old[:, h:h + 1] * c_prev[h] + s_new[:, h:h + 1] * kv[h]
    m_scr[...] = m_new


def _mlstm(proj3, gates3, conv_w, conv_b, gate_b, norm_w):
    b, s, _ = proj3.shape
    L = MLSTM_CHUNK
    qk_w = 2 * M_QK_WIDTH
    return pl.pallas_call(
        _mlstm_kernel,
        grid=(b, s // L),
        in_specs=[
            pl.BlockSpec((1, L, qk_w), lambda i, c: (i, c, 0)),
            pl.BlockSpec((1, SUBLANES, qk_w),
                         lambda i, c: (i, jnp.maximum(c * (L // SUBLANES) - 1, 0), 0)),
            pl.BlockSpec((1, L, M_V_WIDTH), lambda i, c: (i, c, qk_w // M_V_WIDTH)),
            pl.BlockSpec((1, L, M_V_WIDTH), lambda i, c: (i, c, qk_w // M_V_WIDTH + 1)),
            pl.BlockSpec((1, L, GATE_WIDTH), lambda i, c: (i, c, 0)),
            pl.BlockSpec((CONV_WIDTH, qk_w), lambda i, c: (0, 0)),
            pl.BlockSpec((1, qk_w), lambda i, c: (0, 0)),
            pl.BlockSpec((1, GATE_WIDTH), lambda i, c: (0, 0)),
            pl.BlockSpec((1, M_V_WIDTH), lambda i, c: (0, 0)),
        ],
        out_specs=pl.BlockSpec((1, L, M_V_WIDTH), lambda i, c: (i, c, 0)),
        out_shape=jax.ShapeDtypeStruct((b, s, M_V_WIDTH), BF16),
        scratch_shapes=[
            pltpu.VMEM((L + SUBLANES, qk_w), F32),
            pltpu.VMEM((M_HEADS, M_QK_DIM, M_V_DIM + LANES), F32),
            pltpu.VMEM((1, LANES), F32),
        ],
        compiler_params=pltpu.CompilerParams(
            dimension_semantics=("parallel", "arbitrary"), vmem_limit_bytes=VMEM_LIMIT),
        name="mlstm",
    )(proj3, proj3, proj3, proj3, gates3, conv_w, conv_b, gate_b, norm_w)


def _attn_kernel(q_ref, k_ref, v_ref, cos_ref, sin_ref, qw_ref, kw_ref, aw_ref, out_ref,
                 qs, ks, vs, acc, ms, ls):
    R = ATTN_RESIDUES
    s_len = out_ref.shape[1]
    per = s_len // R
    W = ATTN_WINDOW_SUB

    ones = jnp.ones((A_HEAD_DIM, LANES), BF16)
    row_out = lax.broadcasted_iota(jnp.int32, (W, W), 0)
    row_in = lax.broadcasted_iota(jnp.int32, (W, W), 1)
    rows_per_res = W // R
    perm = jnp.where((row_in % R) * rows_per_res + row_in // R == row_out, 1.0, 0.0).astype(BF16)
    slab = ATTN_PREP_ROWS
    n_chunks = slab // W

    def lane_ssq(x):
        sq = x * x
        hi = sq.astype(BF16)
        lo = (sq - hi.astype(F32)).astype(BF16)
        return (jnp.dot(hi, ones, preferred_element_type=F32)
                + jnp.dot(lo, ones, preferred_element_type=F32))

    def prep(ci, carry):
        base = pl.multiple_of(ci * slab, slab)
        rows = pl.ds(base, slab)
        xq, xk = q_ref[0, rows, :], k_ref[0, rows, :]
        ssq_q, ssq_k = lane_ssq(xq), lane_ssq(xk)
        cos, sin = cos_ref[rows, :], sin_ref[rows, :]

        def rope(x, ssq, w):
            xn = x * lax.rsqrt(ssq * (1.0 / A_HEAD_DIM) + NORM_EPS) * w
            return xn * cos + pltpu.roll(xn, A_HEAD_DIM // 2, 1) * sin

        srcs = (
            (rope(xq, ssq_q, qw_ref[...]) * (A_HEAD_DIM ** -0.5 * LOG2_E)).astype(BF16),
            rope(xk, ssq_k, kw_ref[...]).astype(BF16),
            v_ref[0, rows, :].astype(BF16))
        dst_base = pl.multiple_of(ci * (slab // R), slab // R)
        for src, dst in zip(srcs, (qs, ks, vs)):
            for j in range(n_chunks):
                out = jnp.dot(perm, src[j * W:(j + 1) * W, :], preferred_element_type=F32)
                for r in range(R):
                    dst[pl.ds(r * per + dst_base + j * rows_per_res, rows_per_res), :] = (
                        out[r * rows_per_res:(r + 1) * rows_per_res, :])
        return carry

    lax.fori_loop(0, s_len // slab, prep, 0)

    pos_q = lax.broadcasted_iota(jnp.int32, (W, W), 0)
    pos_k = lax.broadcasted_iota(jnp.int32, (W, W), 1)
    nt = (((1,), (1,)), ((), ()))
    G = ATTN_GROUP

    for pi, d in enumerate(ATTN_DILATIONS):
        run = SUBLANES * d
        nruns = W // run
        shift = run.bit_length() - 1

        def sub_pos(pos, run=run, nruns=nruns, shift=shift):
            return (pos & (run - 1)) * nruns + (pos >> shift)

        uq = sub_pos(pos_q)
        uk = sub_pos(pos_k)
        bias_cur = jnp.where(uk <= uq, 0.0, -jnp.inf).astype(F32)
        bias_prev = jnp.where(uk >= uq, 0.0, -jnp.inf).astype(F32)

        def group(gi, carry, d=d, run=run, nruns=nruns, bias_cur=bias_cur, bias_prev=bias_prev,
                  first=(pi == 0)):
            nb = per // run
            parts = max(nb // G, 1)
            if parts == 1:
                res, part = gi * (G // nb), 0
                slab = [pl.multiple_of((res + d * a) * per, per) for a in range(nruns)]
            else:
                res, part = gi // parts, gi % parts
                slab = [pl.multiple_of((res + d * a) * per + part * (G * run), G * run)
                        for a in range(nruns)]

            def blocks(ref):
                slabs = [ref[pl.ds(st, G * run), :] for st in slab]
                return [jnp.concatenate([sl[g * run:(g + 1) * run, :] for sl in slabs], axis=0)
                        for g in range(G)]

            qg = [x.astype(BF16) for x in blocks(qs)]
            kg = [x.astype(BF16) for x in blocks(ks)]
            vg = [x.astype(BF16) for x in blocks(vs)]
            has_prev = [parts > 1 or g % nb > 0 for g in range(G)]
            if parts > 1:
                pst = [pl.multiple_of(jnp.maximum(st - run, 0), run) for st in slab]
                k_before = jnp.concatenate([ks[pl.ds(st, run), :] for st in pst], axis=0).astype(BF16)
                v_before = jnp.concatenate([vs[pl.ds(st, run), :] for st in pst], axis=0).astype(BF16)
                before_bias = jnp.where(part > 0, 0.0, -jnp.inf)
            else:
                k_before = v_before = None
            k_prev =[k_before if g == 0 else kg[g - 1] for g in range(G)]
            v_prev = [v_before if g == 0 else vg[g - 1] for g in range(G)]

            s_cur = [lax.dot_general(qg[g], kg[g], nt, preferred_element_type=F32) for g in range(G)]
            s_prv = [lax.dot_general(qg[g], k_prev[g], nt, preferred_element_type=F32)
                     if has_prev[g] else None for g in range(G)]
            p_cur, p_prv, m_blk, l_blk = [], [], [], []
            for g in range(G):
                s_c = s_cur[g] + bias_cur
                if has_prev[g]:
                    s_p = s_prv[g] + bias_prev
                    if g == 0:
                        s_p = s_p + before_bias
                    m = jnp.max(jnp.maximum(s_c, s_p), axis=1, keepdims=True)
                    p_c = jnp.exp2(s_c - m)
                    p_p = jnp.exp2(s_p - m)
                    l = jnp.sum(p_c + p_p, axis=1, keepdims=True)
                    p_prv.append(p_p.astype(BF16))
                else:
                    m = jnp.max(s_c, axis=1, keepdims=True)
                    p_c = jnp.exp2(s_c - m)
                    l = jnp.sum(p_c, axis=1, keepdims=True)
                    p_prv.append(None)
                p_cur.append(p_c.astype(BF16))
                m_blk.append(m)
                l_blk.append(l)
            o_blk = []
            for g in range(G):
                o = jnp.dot(p_cur[g], vg[g], preferred_element_type=F32)
                if has_prev[g]:
                    o = o + jnp.dot(p_prv[g], v_prev[g], preferred_element_type=F32)
                o_blk.append(o)

            if not first:
                m_old_g, l_old_g, a_old_g = blocks(ms), blocks(ls), blocks(acc)
            a_out, m_out, l_out = [], [], []
            for g in range(G):
                if first:
                    a_out.append(o_blk[g])
                    m_out.append(jnp.broadcast_to(m_blk[g], (W, LANES)))
                    l_out.append(jnp.broadcast_to(l_blk[g], (W, LANES)))
                else:
                    m_new = jnp.maximum(m_old_g[g], m_blk[g])
                    e_old = jnp.exp2(m_old_g[g] - m_new)
                    e_blk = jnp.exp2(m_blk[g] - m_new)
                    a_out.append(a_old_g[g] * e_old + o_blk[g] * e_blk)
                    l_out.append(l_old_g[g] * e_old + l_blk[g] * e_blk)
                    m_out.append(m_new)

            for ref, vals in ((acc, a_out), (ms, m_out), (ls, l_out)):
                for a, st in enumerate(slab):
                    ref[pl.ds(st, G * run), :] = jnp.concatenate(
                        [v[a * run:(a + 1) * run, :] for v in vals], axis=0)
            return carry

        lax.fori_loop(0, s_len // (G * W), group, 0)

    def finish(r, carry):
        src = pl.ds(pl.multiple_of(r * per, per), per)
        o = acc[src, :] / ls[src, :]
        qs[pl.ds(r, per, stride=R), :] = _rms(o, aw_ref[...])
        return carry

    lax.fori_loop(0, R, finish, 0, unroll=4)
    out_ref[0] = qs[...].astype(BF16)


def _attention(proj3, cos_full, sin_signed, q_norm_w, k_norm_w, attn_norm_w):
    b, s, _ = proj3.shape
    base = (2 * M_QK_WIDTH + 2 * M_V_WIDTH) // A_HEAD_DIM
    nh = A_HEADS
    row_spec = lambda off: pl.BlockSpec((1, s, A_HEAD_DIM), lambda i, h: (i, 0, base + off + h))
    return pl.pallas_call(
        _attn_kernel,
        grid=(b, nh),
        in_specs=[
            row_spec(0), row_spec(nh), row_spec(2 * nh),
            pl.BlockSpec((s, A_HEAD_DIM), lambda i, h: (0, 0)),
            pl.BlockSpec((s, A_HEAD_DIM), lambda i, h: (0, 0)),
            pl.BlockSpec((1, A_HEAD_DIM), lambda i, h: (0, 0)),
            pl.BlockSpec((1, A_HEAD_DIM), lambda i, h: (0, 0)),
            pl.BlockSpec((1, A_HEAD_DIM), lambda i, h: (0, h)),
        ],
        out_specs=pl.BlockSpec((1, s, A_HEAD_DIM), lambda i, h: (i, 0, h)),
        out_shape=jax.ShapeDtypeStruct((b, s, A_WIDTH), BF16),
        scratch_shapes=[pltpu.VMEM((s, A_HEAD_DIM), F32) for _ in range(6)],
        compiler_params=pltpu.CompilerParams(
            dimension_semantics=("parallel", "parallel"), vmem_limit_bytes=VMEM_LIMIT),
        name="attention",
    )(proj3, proj3, proj3, cos_full, sin_signed, q_norm_w, k_norm_w, attn_norm_w)


def _out_proj_kernel(hm_ref, ha_ref, w_ref, x_ref, n2_ref, h_ref, u_ref):
    h1 = (x_ref[...]
          + jnp.dot(hm_ref[...], w_ref[:M_V_WIDTH, :], preferred_element_type=F32)
          + jnp.dot(ha_ref[...], w_ref[M_V_WIDTH:, :], preferred_element_type=F32))
    h_ref[...] = h1
    u_ref[...] = _rms(h1, n2_ref[...]).astype(BF16)


def _out_proj(hm, ha, w_out, x2d, n2, tm=512):
    t = x2d.shape[0]
    return pl.pallas_call(
        _out_proj_kernel,
        grid=(t // tm,),
        in_specs=[
            pl.BlockSpec((tm, M_V_WIDTH), lambda i: (i, 0)),
            pl.BlockSpec((tm, A_WIDTH), lambda i: (i, 0)),
            pl.BlockSpec((D_MODEL, D_MODEL), lambda i: (0, 0)),
            pl.BlockSpec((tm, D_MODEL), lambda i: (i, 0)),
            pl.BlockSpec((1, D_MODEL), lambda i: (0, 0)),
        ],
        out_specs=[
            pl.BlockSpec((tm, D_MODEL), lambda i: (i, 0)),
            pl.BlockSpec((tm, D_MODEL), lambda i: (i, 0)),
        ],
        out_shape=[
            jax.ShapeDtypeStruct((t, D_MODEL), F32),
            jax.ShapeDtypeStruct((t, D_MODEL), BF16),
        ],
        compiler_params=pltpu.CompilerParams(
            dimension_semantics=("parallel",), vmem_limit_bytes=VMEM_LIMIT),
        name="out_proj",
    )(hm, ha, w_out, x2d, n2)


def _ffn_act_kernel(u_ref, wg_ref, wu_ref, a_ref, wg_scr, wu_scr):
    @pl.when(pl.program_id(1) == 0)
    def _():
        wg_scr[...] = wg_ref[...].astype(BF16)
        wu_scr[...] = wu_ref[...].astype(BF16)

    u = u_ref[...]
    g = jnp.dot(u, wg_scr[...], preferred_element_type=F32)
    up = jnp.dot(u, wu_scr[...], preferred_element_type=F32)
    a_ref[...] = (g * _sigmoid(g) * up).astype(BF16)


def _ffn_act(u2, wg, wu, tm=1024, tf=512):
    t = u2.shape[0]
    return pl.pallas_call(
        _ffn_act_kernel,
        grid=(D_FF // tf, t // tm),
        in_specs=[
            pl.BlockSpec((tm, D_MODEL), lambda f, i: (i, 0)),
            pl.BlockSpec((D_MODEL, tf), lambda f, i: (0, f)),
            pl.BlockSpec((D_MODEL, tf), lambda f, i: (0, f)),
        ],
        out_specs=pl.BlockSpec((tm, tf), lambda f, i: (i, f)),
        out_shape=jax.ShapeDtypeStruct((t, D_FF), BF16),
        scratch_shapes=[pltpu.VMEM((D_MODEL, tf), BF16), pltpu.VMEM((D_MODEL, tf), BF16)],
        compiler_params=pltpu.CompilerParams(
            dimension_semantics=("parallel", "arbitrary"), vmem_limit_bytes=VMEM_LIMIT),
        name="ffn_act",
    )(u2, wg, wu)


def _ffn_down_kernel(a_ref, wd_ref, h_ref, o_ref):
    o_ref[...] = h_ref[...] + jnp.dot(a_ref[...], wd_ref[...], preferred_element_type=F32)


def _ffn_down(act, wd, h1, tm=512, tn=1024):
    t = act.shape[0]
    return pl.pallas_call(
        _ffn_down_kernel,
        grid=(D_MODEL // tn, t // tm),
        in_specs=[
            pl.BlockSpec((tm, D_FF), lambda j, i: (i, 0)),
            pl.BlockSpec((D_FF, tn), lambda j, i: (0, j)),
            pl.BlockSpec((tm, tn), lambda j, i: (i, j)),
        ],
        out_specs=pl.BlockSpec((tm, tn), lambda j, i: (i, j)),
        out_shape=jax.ShapeDtypeStruct((t, D_MODEL), F32),
        compiler_params=pltpu.CompilerParams(
            dimension_semantics=("parallel", "parallel"), vmem_limit_bytes=VMEM_LIMIT),
        name="ffn_down",
    )(act, wd, h1)


def _rope_tables(s):
    f32 = np.float32
    inv_freq = f32(ROPE_THETA) ** (-np.arange(0, A_HEAD_DIM, 2, dtype=f32) / f32(A_HEAD_DIM))
    ang = np.arange(s, dtype=f32)[:, None] * inv_freq[None, :]
    cos, sin = np.cos(ang).astype(f32), np.sin(ang).astype(f32)

    return (jnp.asarray(np.concatenate([cos, cos], axis=1)),
            jnp.asarray(np.concatenate([-sin, sin], axis=1)))


def _pad_lanes(a, width):
    return jnp.pad(a, ((0, 0), (0, width - a.shape[1])))


def kernel(x, norm1_w, w_in, conv_w, conv_b, igate_b, fgate_b, q_norm_w, k_norm_w,
           mlstm_norm_w, attn_norm_w, w_out, norm2_w, w_gate, w_up, w_down):
    b, s, d = x.shape
    depth = norm1_w.shape[0]
    cos_full, sin_signed = _rope_tables(s)
    h = x
    for l in range(depth):
        w_main, w_gates = _w_in_prep(w_in[l].T)
        w_main = w_main.reshape(PROJ_WIDTH, d)
        gate_b = jnp.concatenate(
            [_pad_lanes(igate_b[l][None, :], LANES), _pad_lanes(fgate_b[l][None, :], LANES)], axis=1)

        x2d = h.reshape(b * s, d)
        proj, gates = _in_proj(x2d, norm1_w[l][None, :], w_main, w_gates)
        proj3 = proj.reshape(b, s, PROJ_WIDTH)
        gates3 = gates.reshape(b, s, GATE_WIDTH)

        hm = _mlstm(proj3, gates3, conv_w[l], conv_b[l][None, :], gate_b,
                    mlstm_norm_w[l].reshape(1, M_V_WIDTH))
        ha = _attention(proj3, cos_full, sin_signed, q_norm_w[l][None, :], k_norm_w[l][None, :],
                        attn_norm_w[l].reshape(1, A_WIDTH))

        h1, u2 = _out_proj(hm.reshape(b * s, M_V_WIDTH), ha.reshape(b * s, A_WIDTH),
                           w_out[l].astype(BF16), x2d, norm2_w[l][None, :])
        act = _ffn_act(u2, w_gate[l], w_up[l])
        out = _ffn_down(act, w_down[l].astype(BF16), h1)
        h = out.reshape(b, s, d)
    return h
```

```python
import numpy as np
import jax
import jax.numpy as jnp
from jax import lax
from jax.experimental import pallas as pl
from jax.experimental.pallas import tpu as pltpu

D_MODEL = 2048
M_HEADS = 4
A_HEADS = 8
M_V_DIM = 256
M_QK_DIM = 128
A_HEAD_DIM = 128
CONV_WIDTH = 4
GATE_CAP = 15.0
ROPE_THETA = 10000.0
NORM_EPS = 1e-6
LOG2_E = 1.4426950408889634
D_FF = 5632
M_QK_WIDTH = M_HEADS * M_QK_DIM
M_V_WIDTH = M_HEADS * M_V_DIM
A_WIDTH = A_HEADS * A_HEAD_DIM
PROJ_WIDTH = 2 * M_QK_WIDTH + 2 * M_V_WIDTH + 3 * A_WIDTH
GATE_WIDTH = 256

LANES = 128
SUBLANES = 8
VMEM_LIMIT = 56 * 1024 * 1024
VMEM_LIMIT_WIDE = 60 * 1024 * 1024

MLSTM_CHUNK = 256
ATTN_WINDOW_SUB = 128
ATTN_DILATIONS = (16, 4, 1)
ATTN_RESIDUES = 16
ATTN_GROUP = 16
ATTN_PREP_ROWS = 1024

BF16 = jnp.bfloat16
F32 = jnp.float32


def _sigmoid(x):
    return 1.0 / (1.0 + jnp.exp(-x))


def _rms(x, w):
    return x * lax.rsqrt(jnp.mean(x * x, axis=-1, keepdims=True) + NORM_EPS) * w


def _w_in_prep_kernel(left_ref, tail_ref, g_ref, main_ref, gate_ref):
    main_ref[0] = left_ref[...].astype(BF16)
    main_ref[1] = tail_ref[...].astype(BF16)

    @pl.when(pl.program_id(0) == 0)
    def _():
        g = g_ref[...]
        gate_ref[...] = jnp.zeros_like(gate_ref)
        row = lax.broadcasted_iota(jnp.int32, g.shape, 0)
        zero = jnp.zeros_like(g)
        gate_ref[0:2 * SUBLANES, :] = jnp.concatenate(
            [jnp.where(row < M_HEADS, g, 0.0), zero], axis=0).astype(BF16)
        gate_ref[LANES:LANES + 2 * SUBLANES, :] = jnp.concatenate(
            [jnp.where(row < M_HEADS, pltpu.roll(g, M_HEADS, 0), 0.0), zero], axis=0).astype(BF16)


def _w_in_prep(w_t, tr=256):
    n, k = w_t.shape
    g0 = 2 * M_QK_WIDTH + 2 * M_V_WIDTH
    half = PROJ_WIDTH // 2
    assert g0 == half and n == PROJ_WIDTH + 2 * M_HEADS
    return pl.pallas_call(
        _w_in_prep_kernel,
        grid=(half // tr,),
        in_specs=[
            pl.BlockSpec((tr, k), lambda i: (i, 0)),
            pl.BlockSpec((pl.Element(tr), pl.Element(k)),
                         lambda i: (pl.multiple_of(g0 + 2 * M_HEADS + i * tr, SUBLANES), 0)),
            pl.BlockSpec((pl.Element(2 * M_HEADS), pl.Element(k)), lambda i: (g0, 0)),
        ],
        out_specs=[
            pl.BlockSpec((2, tr, k), lambda i: (0, i, 0)),
            pl.BlockSpec((GATE_WIDTH, k), lambda i: (0, 0)),
        ],
        out_shape=[
            jax.ShapeDtypeStruct((2, half, k), BF16),
            jax.ShapeDtypeStruct((GATE_WIDTH, k), BF16),
        ],
        compiler_params=pltpu.CompilerParams(
            dimension_semantics=("arbitrary",), vmem_limit_bytes=VMEM_LIMIT),
        name="w_in_prep",
    )(w_t, w_t, w_t)


_NT = (((1,), (1,)), ((), ()))


def _in_proj_kernel(x_ref, n1_ref, w_ref, wg_ref, o_ref, og_ref, u_scr):
    @pl.when(pl.program_id(1) == 0)
    def _():
        u = _rms(x_ref[...], n1_ref[...]).astype(BF16)
        u_scr[...] = u
        og_ref[...] = lax.dot_general(u, wg_ref[...], _NT, preferred_element_type=F32)

    o_ref[...] = lax.dot_general(u_scr[...], w_ref[...], _NT, preferred_element_type=F32)


def _in_proj(x2d, n1, w_main, w_gate, tm=1024, tn=2048):
    t = x2d.shape[0]
    return pl.pallas_call(
        _in_proj_kernel,
        grid=(t // tm, PROJ_WIDTH // tn),
        in_specs=[
            pl.BlockSpec((tm, D_MODEL), lambda i, j: (i, 0)),
            pl.BlockSpec((1, D_MODEL), lambda i, j: (0, 0)),
            pl.BlockSpec((tn, D_MODEL), lambda i, j: (j, 0)),
            pl.BlockSpec((GATE_WIDTH, D_MODEL), lambda i, j: (0, 0)),
        ],
        out_specs=[
            pl.BlockSpec((tm, tn), lambda i, j: (i, j)),
            pl.BlockSpec((tm, GATE_WIDTH), lambda i, j: (i, 0)),
        ],
        out_shape=[
            jax.ShapeDtypeStruct((t, PROJ_WIDTH), F32),
            jax.ShapeDtypeStruct((t, GATE_WIDTH), F32),
        ],
        scratch_shapes=[pltpu.VMEM((tm, D_MODEL), BF16)],
        compiler_params=pltpu.CompilerParams(
            dimension_semantics=("parallel", "arbitrary"), vmem_limit_bytes=VMEM_LIMIT_WIDE),
        name="in_proj",
    )(x2d, n1, w_main, w_gate)


def _mlstm_kernel(qk_ref, qkprev_ref, v_ref, o_ref, g_ref, cw_ref, cb_ref, gb_ref, nw_ref,
                  out_ref, xbuf, c_scr, m_scr):
    L = MLSTM_CHUNK
    c = pl.program_id(1)

    @pl.when(c == 0)
    def _():
        c_scr[...] = jnp.zeros_like(c_scr)
        m_scr[...] = jnp.zeros_like(m_scr)

    xbuf[0:SUBLANES, :] = jnp.where(c > 0, qkprev_ref[0], 0.0)
    xbuf[SUBLANES:, :] = qk_ref[0]
    y = cb_ref[...]
    for j in range(CONV_WIDTH):
        y = y + cw_ref[j:j + 1, :] * xbuf[pl.ds(SUBLANES - (CONV_WIDTH - 1) + j, L), :]
    qk = y * _sigmoid(y)

    gt = g_ref[0]
    i_pre = (GATE_CAP * LOG2_E) * jnp.tanh((gt[:, :LANES] + gb_ref[:, :LANES]) / GATE_CAP)
    f_pre = GATE_CAP * jnp.tanh((gt[:, LANES:] + gb_ref[:, LANES:]) / GATE_CAP)
    log_f = (jnp.minimum(f_pre, 0.0) - jnp.log1p(jnp.exp(-jnp.abs(f_pre)))) * LOG2_E

    row = lax.broadcasted_iota(jnp.int32, (L, L), 0)
    col = lax.broadcasted_iota(jnp.int32, (L, L), 1)
    causal = col <= row
    tril = jnp.where(causal, 1.0, 0.0).astype(BF16)
    hi = log_f.astype(BF16)
    r1 = log_f - hi.astype(F32)
    mid = r1.astype(BF16)
    lo = (r1 - mid.astype(F32)).astype(BF16)
    cum = (jnp.dot(tril, hi, preferred_element_type=F32)
           + jnp.dot(tril, mid, preferred_element_type=F32)
           + jnp.dot(tril, lo, preferred_element_type=F32))

    g_tot = cum[L - 1:L, :]
    a = g_tot - cum + i_pre
    m_loc = jnp.max(a, axis=0, keepdims=True)
    w_state = jnp.exp2(a - m_loc)
    m_prev = m_scr[...]
    m_new = jnp.maximum(g_tot + m_prev, m_loc)
    s_old = jnp.exp2(g_tot + m_prev - m_new)
    s_new = jnp.exp2(m_loc - m_new)
    inter_log = cum + m_prev
    b_t = (i_pre - cum).T

    lane = lax.broadcasted_iota(jnp.int32, (L, LANES), 1)
    one_col = jnp.where(lane == 0, 1.0, 0.0).astype(BF16)
    vall = v_ref[0]
    oall = o_ref[0]
    heads = range(M_HEADS)
    vsl = [slice(h * M_V_DIM, (h + 1) * M_V_DIM) for h in heads]

    q = [(qk[:, h * M_QK_DIM:(h + 1) * M_QK_DIM] * (M_QK_DIM ** -0.5)).astype(BF16) for h in heads]
    kf = [qk[:, M_QK_WIDTH + h * M_QK_DIM:M_QK_WIDTH + (h + 1) * M_QK_DIM] for h in heads]
    v_aug = [jnp.concatenate([vall[:, vsl[h]].astype(BF16), one_col], axis=1) for h in heads]
    c_prev = [c_scr[h] for h in heads]
    s = [lax.dot_general(q[h], kf[h].astype(BF16), _NT, preferred_element_type=F32) for h in heads]
    inter = [jnp.dot(q[h], c_prev[h].astype(BF16), preferred_element_type=F32) for h in heads]
    kv = [jnp.dot((kf[h] * w_state[:, h:h + 1]).T.astype(BF16), v_aug[h],
                  preferred_element_type=F32) for h in heads]

    m_t, w_inter, s_qk = [], [], []
    for h in heads:
        d_log = jnp.where(causal, cum[:, h:h + 1] + b_t[h:h + 1, :], -jnp.inf)
        il = inter_log[:, h:h + 1]
        m = jnp.maximum(il, jnp.max(d_log, axis=1, keepdims=True))
        m_t.append(m)
        w_inter.append(jnp.exp2(il - m))
        s_qk.append((s[h] * jnp.exp2(d_log - m)).astype(BF16))

    intra = [jnp.dot(s_qk[h], v_aug[h], preferred_element_type=F32) for h in heads]

    for h in heads:
        tot = w_inter[h] * inter[h] + intra[h]
        num = tot[:, :M_V_DIM]
        den = tot[:, M_V_DIM:M_V_DIM + 1]
        hout = num / jnp.maximum(jnp.abs(den), jnp.exp2(-m_t[h]))
        hn = _rms(hout, nw_ref[:, vsl[h]])
        out_ref[0, :, vsl[h]] = (hn * _sigmoid(oall[:, vsl[h]])).astype(BF16)
        c_scr[h] = s_old[:, h:h + 1] * c_prev[h] + s_new[:, h:h + 1] * kv[h]
    m_scr[...] = m_new


def _mlstm(proj3, gates3, conv_w, conv_b, gate_b, norm_w):
    b, s, _ = proj3.shape
    L = MLSTM_CHUNK
    qk_w = 2 * M_QK_WIDTH
    return pl.pallas_call(
        _mlstm_kernel,
        grid=(b, s // L),
        in_specs=[
            pl.BlockSpec((1, L, qk_w), lambda i, c: (i, c, 0)),
            pl.BlockSpec((1, SUBLANES, qk_w),
                         lambda i, c: (i, jnp.maximum(c * (L // SUBLANES) - 1, 0), 0)),
            pl.BlockSpec((1, L, M_V_WIDTH), lambda i, c: (i, c, qk_w // M_V_WIDTH)),
            pl.BlockSpec((1, L, M_V_WIDTH), lambda i, c: (i, c, qk_w // M_V_WIDTH + 1)),
            pl.BlockSpec((1, L, GATE_WIDTH), lambda i, c: (i, c, 0)),
            pl.BlockSpec((CONV_WIDTH, qk_w), lambda i, c: (0, 0)),
            pl.BlockSpec((1, qk_w), lambda i, c: (0, 0)),
            pl.BlockSpec((1, GATE_WIDTH), lambda i, c: (0, 0)),
            pl.BlockSpec((1, M_V_WIDTH), lambda i, c: (0, 0)),
        ],
        out_specs=pl.BlockSpec((1, L, M_V_WIDTH), lambda i, c: (i, c, 0)),
        out_shape=jax.ShapeDtypeStruct((b, s, M_V_WIDTH), BF16),
        scratch_shapes=[
            pltpu.VMEM((L + SUBLANES, qk_w), F32),
            pltpu.VMEM((M_HEADS, M_QK_DIM, M_V_DIM + LANES), F32),
            pltpu.VMEM((1, LANES), F32),
        ],
        compiler_params=pltpu.CompilerParams(
            dimension_semantics=("parallel", "arbitrary"), vmem_limit_bytes=VMEM_LIMIT),
        name="mlstm",
    )(proj3, proj3, proj3, proj3, gates3, conv_w, conv_b, gate_b, norm_w)


def _attn_kernel(q_ref, k_ref, v_ref, cos_ref, sin_ref, qw_ref, kw_ref, aw_ref, out_ref,
                 qs, ks, vs, acc, ms, ls):
    R = ATTN_RESIDUES
    s_len = out_ref.shape[1]
    per = s_len // R
    W = ATTN_WINDOW_SUB

    ones = jnp.ones((A_HEAD_DIM, LANES), BF16)
    row_out = lax.broadcasted_iota(jnp.int32, (W, W), 0)
    row_in = lax.broadcasted_iota(jnp.int32, (W, W), 1)
    rows_per_res = W // R
    perm = jnp.where((row_in % R) * rows_per_res + row_in // R == row_out, 1.0, 0.0).astype(BF16)
    unperm = jnp.where((row_out % R) * rows_per_res + row_out // R == row_in, 1.0, 0.0).astype(BF16)
    slab = ATTN_PREP_ROWS
    n_chunks = slab // W

    def lane_ssq(x):
        sq = x * x
        hi = sq.astype(BF16)
        lo = (sq - hi.astype(F32)).astype(BF16)
        return (jnp.dot(hi, ones, preferred_element_type=F32)
                + jnp.dot(lo, ones, preferred_element_type=F32))

    def prep(ci, carry):
        base = pl.multiple_of(ci * slab, slab)
        rows = pl.ds(base, slab)
        xq, xk = q_ref[0, rows, :], k_ref[0, rows, :]
        ssq_q, ssq_k = lane_ssq(xq), lane_ssq(xk)
        cos, sin = cos_ref[rows, :], sin_ref[rows, :]

        def rope(x, ssq, w):
            xn = x * lax.rsqrt(ssq * (1.0 / A_HEAD_DIM) + NORM_EPS) * w
            return xn * cos + pltpu.roll(xn, A_HEAD_DIM // 2, 1) * sin

        srcs = (
            (rope(xq, ssq_q, qw_ref[...]) * (A_HEAD_DIM ** -0.5 * LOG2_E)).astype(BF16),
            rope(xk, ssq_k, kw_ref[...]).astype(BF16),
            v_ref[0, rows, :].astype(BF16))
        dst_base = pl.multiple_of(ci * (slab // R), slab // R)
        for src, dst in zip(srcs, (qs, ks, vs)):
            for j in range(n_chunks):
                out = jnp.dot(perm, src[j * W:(j + 1) * W, :], preferred_element_type=F32)
                for r in range(R):
                    dst[pl.ds(r * per + dst_base + j * rows_per_res, rows_per_res), :] = (
                        out[r * rows_per_res:(r + 1) * rows_per_res, :])
        return carry

    lax.fori_loop(0, s_len // slab, prep, 0)

    pos_q = lax.broadcasted_iota(jnp.int32, (W, W), 0)
    pos_k = lax.broadcasted_iota(jnp.int32, (W, W), 1)
    nt = (((1,), (1,)), ((), ()))
    G = ATTN_GROUP

    for pi, d in enumerate(ATTN_DILATIONS):
        run = SUBLANES * d
        nruns = W // run
        shift = run.bit_length() - 1

        def sub_pos(pos, run=run, nruns=nruns, shift=shift):
            return (pos & (run - 1)) * nruns + (pos >> shift)

        uq = sub_pos(pos_q)
        uk = sub_pos(pos_k)
        bias_cur = jnp.where(uk <= uq, 0.0, -jnp.inf).astype(F32)
        bias_prev = jnp.where(uk >= uq, 0.0, -jnp.inf).astype(F32)

        def group(gi, carry, d=d, run=run, nruns=nruns, bias_cur=bias_cur, bias_prev=bias_prev,
                  first=(pi == 0), last=(pi == len(ATTN_DILATIONS) - 1)):
            nb = per // run
            parts = max(nb // G, 1)
            if parts == 1:
                res, part = gi * (G // nb), 0
                slab = [pl.multiple_of((res + d * a) * per, per) for a in range(nruns)]
            else:
                res, part = gi // parts, gi % parts
                slab = [pl.multiple_of((res + d * a) * per + part * (G * run), G * run)
                        for a in range(nruns)]

            def blocks(ref):
                slabs = [ref[pl.ds(st, G * run), :] for st in slab]
                return [jnp.concatenate([sl[g * run:(g + 1) * run, :] for sl in slabs], axis=0)
                        for g in range(G)]

            qg = [x.astype(BF16) for x in blocks(qs)]
            kg = [x.astype(BF16) for x in blocks(ks)]
            vg = [x.astype(BF16) for x in blocks(vs)]
            has_prev = [parts > 1 or g % nb > 0 for g in range(G)]
            if parts > 1:
                pst = [pl.multiple_of(jnp.maximum(st - run, 0), run) for st in slab]
                k_before = jnp.concatenate([ks[pl.ds(st, run), :] for st in pst], axis=0).astype(BF16)
                v_before = jnp.concatenate([vs[pl.ds(st, run), :] for st in pst], axis=0).astype(BF16)
                before_bias = jnp.where(part > 0, 0.0, -jnp.inf)
            else:
                k_before = v_before = None
            k_prev =[k_before if g == 0 else kg[g - 1] for g in range(G)]
            v_prev = [v_before if g == 0 else vg[g - 1] for g in range(G)]

            s_cur = [lax.dot_general(qg[g], kg[g], nt, preferred_element_type=F32) for g in range(G)]
            s_prv = [lax.dot_general(qg[g], k_prev[g], nt, preferred_element_type=F32)
                     if has_prev[g] else None for g in range(G)]
            p_cur, p_prv, m_blk, l_blk = [], [], [], []
            for g in range(G):
                s_c = s_cur[g] + bias_cur
                if has_prev[g]:
                    s_p = s_prv[g] + bias_prev
                    if g == 0:
                        s_p = s_p + before_bias
                    m = jnp.max(jnp.maximum(s_c, s_p), axis=1, keepdims=True)
                    p_c = jnp.exp2(s_c - m)
                    p_p = jnp.exp2(s_p - m)
                    l = jnp.sum(p_c + p_p, axis=1, keepdims=True)
                    p_prv.append(p_p.astype(BF16))
                else:
                    m = jnp.max(s_c, axis=1, keepdims=True)
                    p_c = jnp.exp2(s_c - m)
                    l = jnp.sum(p_c, axis=1, keepdims=True)
                    p_prv.append(None)
                p_cur.append(p_c.astype(BF16))
                m_blk.append(m)
                l_blk.append(l)
            o_blk = []
            for g in range(G):
                o = jnp.dot(p_cur[g], vg[g], preferred_element_type=F32)
                if has_prev[g]:
                    o = o + jnp.dot(p_prv[g], v_prev[g], preferred_element_type=F32)
                o_blk.append(o)

            if not first:
                m_old_g, l_old_g, a_old_g = blocks(ms), blocks(ls), blocks(acc)
            a_out, m_out, l_out = [], [], []
            for g in range(G):
                if first:
                    a_out.append(o_blk[g])
                    m_out.append(jnp.broadcast_to(m_blk[g], (W, LANES)))
                    l_out.append(jnp.broadcast_to(l_blk[g], (W, LANES)))
                else:
                    m_new = jnp.maximum(m_old_g[g], m_blk[g])
                    e_old = jnp.exp2(m_old_g[g] - m_new)
                    e_blk = jnp.exp2(m_blk[g] - m_new)
                    a_out.append(a_old_g[g] * e_old + o_blk[g] * e_blk)
                    l_out.append(l_old_g[g] * e_old + l_blk[g] * e_blk)
                    m_out.append(m_new)

            if not last:
                for ref, vals in ((acc, a_out), (ms, m_out), (ls, l_out)):
                    for a, st in enumerate(slab):
                        ref[pl.ds(st, G * run), :] = jnp.concatenate(
                            [v[a * run:(a + 1) * run, :] for v in vals], axis=0)
                return carry

            assert d == 1
            normed = [_rms(a_out[g] / l_out[g], aw_ref[...]).astype(BF16) for g in range(G)]
            tokens = [jnp.dot(unperm, x, preferred_element_type=F32) for x in normed]
            for g in range(G):
                out_ref[0, pl.ds(pl.multiple_of((gi * G + g) * W, W), W), :] = tokens[g].astype(BF16)
            return carry

        lax.fori_loop(0, s_len // (G * W), group, 0)


def _attention(proj3, cos_full, sin_signed, q_norm_w, k_norm_w, attn_norm_w):
    b, s, _ = proj3.shape
    base = (2 * M_QK_WIDTH + 2 * M_V_WIDTH) // A_HEAD_DIM
    nh = A_HEADS
    row_spec = lambda off: pl.BlockSpec((1, s, A_HEAD_DIM), lambda i, h: (i, 0, base + off + h))
    return pl.pallas_call(
        _attn_kernel,
        grid=(b, nh),
        in_specs=[
            row_spec(0), row_spec(nh), row_spec(2 * nh),
            pl.BlockSpec((s, A_HEAD_DIM), lambda i, h: (0, 0)),
            pl.BlockSpec((s, A_HEAD_DIM), lambda i, h: (0, 0)),
            pl.BlockSpec((1, A_HEAD_DIM), lambda i, h: (0, 0)),
            pl.BlockSpec((1, A_HEAD_DIM), lambda i, h: (0, 0)),
            pl.BlockSpec((1, A_HEAD_DIM), lambda i, h: (0, h)),
        ],
        out_specs=pl.BlockSpec((1, s, A_HEAD_DIM), lambda i, h: (i, 0, h)),
        out_shape=jax.ShapeDtypeStruct((b, s, A_WIDTH), BF16),
        scratch_shapes=[pltpu.VMEM((s, A_HEAD_DIM), F32) for _ in range(6)],
        compiler_params=pltpu.CompilerParams(
            dimension_semantics=("parallel", "parallel"), vmem_limit_bytes=VMEM_LIMIT),
        name="attention",
    )(proj3, proj3, proj3, cos_full, sin_signed, q_norm_w, k_norm_w, attn_norm_w)


def _out_proj_kernel(hm_ref, ha_ref, w_ref, x_ref, n2_ref, h_ref, u_ref):
    h1 = (x_ref[...]
          + jnp.dot(hm_ref[...], w_ref[:M_V_WIDTH, :], preferred_element_type=F32)
          + jnp.dot(ha_ref[...], w_ref[M_V_WIDTH:, :], preferred_element_type=F32))
    h_ref[...] = h1
    u_ref[...] = _rms(h1, n2_ref[...]).astype(BF16)


def _out_proj(hm, ha, w_out, x2d, n2, tm=512):
    t = x2d.shape[0]
    return pl.pallas_call(
        _out_proj_kernel,
        grid=(t // tm,),
        in_specs=[
            pl.BlockSpec((tm, M_V_WIDTH), lambda i: (i, 0)),
            pl.BlockSpec((tm, A_WIDTH), lambda i: (i, 0)),
            pl.BlockSpec((D_MODEL, D_MODEL), lambda i: (0, 0)),
            pl.BlockSpec((tm, D_MODEL), lambda i: (i, 0)),
            pl.BlockSpec((1, D_MODEL), lambda i: (0, 0)),
        ],
        out_specs=[
            pl.BlockSpec((tm, D_MODEL), lambda i: (i, 0)),
            pl.BlockSpec((tm, D_MODEL), lambda i: (i, 0)),
        ],
        out_shape=[
            jax.ShapeDtypeStruct((t, D_MODEL), F32),
            jax.ShapeDtypeStruct((t, D_MODEL), BF16),
        ],
        compiler_params=pltpu.CompilerParams(
            dimension_semantics=("parallel",), vmem_limit_bytes=VMEM_LIMIT),
        name="out_proj",
    )(hm, ha, w_out, x2d, n2)


def _ffn_act_kernel(u_ref, wg_ref, wu_ref, a_ref, wg_scr, wu_scr):
    @pl.when(pl.program_id(1) == 0)
    def _():
        wg_scr[...] = wg_ref[...].astype(BF16)
        wu_scr[...] = wu_ref[...].astype(BF16)

    u = u_ref[...]
    g = jnp.dot(u, wg_scr[...], preferred_element_type=F32)
    up = jnp.dot(u, wu_scr[...], preferred_element_type=F32)
    a_ref[...] = (g * _sigmoid(g) * up).astype(BF16)


def _ffn_act(u2, wg, wu, tm=1024, tf=512):
    t = u2.shape[0]
    return pl.pallas_call(
        _ffn_act_kernel,
        grid=(D_FF // tf, t // tm),
        in_specs=[
            pl.BlockSpec((tm, D_MODEL), lambda f, i: (i, 0)),
            pl.BlockSpec((D_MODEL, tf), lambda f, i: (0, f)),
            pl.BlockSpec((D_MODEL, tf), lambda f, i: (0, f)),
        ],
        out_specs=pl.BlockSpec((tm, tf), lambda f, i: (i, f)),
        out_shape=jax.ShapeDtypeStruct((t, D_FF), BF16),
        scratch_shapes=[pltpu.VMEM((D_MODEL, tf), BF16), pltpu.VMEM((D_MODEL, tf), BF16)],
        compiler_params=pltpu.CompilerParams(
            dimension_semantics=("parallel", "arbitrary"), vmem_limit_bytes=VMEM_LIMIT),
        name="ffn_act",
    )(u2, wg, wu)


def _ffn_down_kernel(a_ref, wd_ref, h_ref, o_ref):
    o_ref[...] = h_ref[...] + jnp.dot(a_ref[...], wd_ref[...], preferred_element_type=F32)


def _ffn_down(act, wd, h1, tm=512, tn=1024):
    t = act.shape[0]
    return pl.pallas_call(
        _ffn_down_kernel,
        grid=(D_MODEL // tn, t // tm),
        in_specs=[
            pl.BlockSpec((tm, D_FF), lambda j, i: (i, 0)),
            pl.BlockSpec((D_FF, tn), lambda j, i: (0, j)),
            pl.BlockSpec((tm, tn), lambda j, i: (i, j)),
        ],
        out_specs=pl.BlockSpec((tm, tn), lambda j, i: (i, j)),
        out_shape=jax.ShapeDtypeStruct((t, D_MODEL), F32),
        compiler_params=pltpu.CompilerParams(
            dimension_semantics=("parallel", "parallel"), vmem_limit_bytes=VMEM_LIMIT),
        name="ffn_down",
    )(act, wd, h1)


def _rope_tables(s):
    f32 = np.float32
    inv_freq = f32(ROPE_THETA) ** (-np.arange(0, A_HEAD_DIM, 2, dtype=f32) / f32(A_HEAD_DIM))
    ang = np.arange(s, dtype=f32)[:, None] * inv_freq[None, :]
    cos, sin = np.cos(ang).astype(f32), np.sin(ang).astype(f32)

    return (jnp.asarray(np.concatenate([cos, cos], axis=1)),
            jnp.asarray(np.concatenate([-sin, sin], axis=1)))


def _pad_lanes(a, width):
    return jnp.pad(a, ((0, 0), (0, width - a.shape[1])))


def kernel(x, norm1_w, w_in, conv_w, conv_b, igate_b, fgate_b, q_norm_w, k_norm_w,
           mlstm_norm_w, attn_norm_w, w_out, norm2_w, w_gate, w_up, w_down):
    b, s, d = x.shape
    depth = norm1_w.shape[0]
    cos_full, sin_signed = _rope_tables(s)
    h = x
    for l in range(depth):
        w_main, w_gates = _w_in_prep(w_in[l].T)
        w_main = w_main.reshape(PROJ_WIDTH, d)
        gate_b = jnp.concatenate(
            [_pad_lanes(igate_b[l][None, :], LANES), _pad_lanes(fgate_b[l][None, :], LANES)], axis=1)

        x2d = h.reshape(b * s, d)
        proj, gates = _in_proj(x2d, norm1_w[l][None, :], w_main, w_gates)
        proj3 = proj.reshape(b, s, PROJ_WIDTH)
        gates3 = gates.reshape(b, s, GATE_WIDTH)

        hm = _mlstm(proj3, gates3, conv_w[l], conv_b[l][None, :], gate_b,
                    mlstm_norm_w[l].reshape(1, M_V_WIDTH))
        ha = _attention(proj3, cos_full, sin_signed, q_norm_w[l][None, :], k_norm_w[l][None, :],
                        attn_norm_w[l].reshape(1, A_WIDTH))

        h1, u2 = _out_proj(hm.reshape(b * s, M_V_WIDTH), ha.reshape(b * s, A_WIDTH),
                           w_out[l].astype(BF16), x2d, norm2_w[l][None, :])
        act = _ffn_act(u2, w_gate[l], w_up[l])
        out = _ffn_down(act, w_down[l].astype(BF16), h1)
        h = out.reshape(b, s, d)
    return h
```
